```python
import math
import jax, jax.numpy as jnp
from jax import lax
import numpy as np

D_MODEL = 1024
BATCH = 4
SEQ = 4096
DEPTH = 4
DEC_BATCH = 32
DEC_SEQ = 1
PAST_LEN = 8192
PAGE_SIZE = 128

N_MIXERS = 3
HEAD_DIM = 64
ATTN_SCALE = 1.0 / math.sqrt(HEAD_DIM)
BLK = 128
A_WINDOWS = (128, 512, 2048)
A_DILATIONS = (1, 4, 16)
A_GROUPS = len(A_WINDOWS)
A_HEADS = 16
A_INNER = A_HEADS * HEAD_DIM
B_WIDTH = D_MODEL
B_GROUPS = 8
B_GROUP_DIM = B_WIDTH // B_GROUPS
CHUNK = 128
C_HEADS = 16
C_INNER = C_HEADS * HEAD_DIM
SB_QK_OFFSET = 6.3
D_FF = 4 * D_MODEL
PLE_DIM = 256
EPS = 1e-6
N_A_LAYERS = (DEPTH + 2) // 3
N_B_LAYERS = (DEPTH + 1) // 3
N_C_LAYERS = DEPTH // 3
F32 = jnp.float32

kernel_name = 'hybrid_dilated_sgu_stickbreak_step'


def rmsnorm(x, g):
    xf = x.astype(F32)
    y = xf * lax.rsqrt(jnp.mean(xf * xf, axis=-1, keepdims=True) + EPS)
    return (y * g.astype(F32)).astype(x.dtype)


def layernorm(x, g, b):
    xf = x.astype(F32)
    mu = jnp.mean(xf, axis=-1, keepdims=True)
    var = jnp.mean(jnp.square(xf - mu), axis=-1, keepdims=True)
    return ((xf - mu) * lax.rsqrt(var + EPS) * g.astype(F32) + b.astype(F32)).astype(x.dtype)


def masked_softmax(s, valid):
    s = jnp.where(valid, s, -jnp.inf)
    m = jnp.max(s, axis=-1, keepdims=True)
    e = jnp.exp(s - m)
    den = jnp.sum(e, axis=-1, keepdims=True)
    return e / den, (m + jnp.log(den))[..., 0]


def merge_groups(outs, lses):
    w = jax.nn.softmax(jnp.stack(lses), axis=0)
    o = jnp.stack(outs)
    return jnp.sum(o * w[..., None].astype(o.dtype), axis=0)


def banded_causal_attn(q, k, v, steps):
    N, L, H, Dh = q.shape
    nb = -(-L // BLK)
    pad = nb * BLK - L
    qb = jnp.pad(q, ((0, 0), (0, pad), (0, 0), (0, 0))).reshape(N, nb, BLK, H, Dh)

    def band(t):
        tp = jnp.pad(t, ((0, 0), (BLK, pad), (0, 0), (0, 0))).reshape(N, nb + 1, BLK, H, Dh)
        return jnp.concatenate([tp[:, :-1], tp[:, 1:]], axis=2)

    kb, vb = band(k), band(v)
    s = jnp.einsum('nbqhd,nbkhd->nbhqk', qb, kb).astype(F32) * ATTN_SCALE
    qi = jnp.arange(BLK)[:, None]
    kj = jnp.arange(2 * BLK)[None, :]
    dist = BLK + qi - kj
    blk = jnp.arange(nb)[:, None, None]
    valid = (dist >= 0) & (dist <= steps) & ((blk > 0) | (kj >= BLK))
    p, lse = masked_softmax(s, valid[None, :, None])
    o = jnp.einsum('nbhqk,nbkhd->nbqhd', p.astype(vb.dtype), vb)
    o = o.reshape(N, nb * BLK, H, Dh)[:, :L]
    lse = jnp.swapaxes(lse, 2, 3).reshape(N, nb * BLK, H)[:, :L]
    return o, lse


def to_streams(t, dil):
    B_, S_, H, Dh = t.shape
    return t.reshape(B_, S_ // dil, dil, H, Dh).transpose(0, 2, 1, 3, 4).reshape(B_ * dil, S_ // dil, H, Dh)


def from_streams(t, B_, dil):
    L = t.shape[1]
    rest = t.shape[2:]
    return t.reshape((B_, dil, L) + rest).swapaxes(1, 2).reshape((B_, L * dil) + rest)


def dilated_prompt(h, w_in, w_out):
    B_, S_, _ = h.shape
    qkv = (h @ w_in).reshape(B_, S_, 3, A_GROUPS, A_HEADS, HEAD_DIM)
    outs, lses, rows = [], [], []
    for g, (win, dil) in enumerate(zip(A_WINDOWS, A_DILATIONS)):
        q, k, v = qkv[:, :, 0, g], qkv[:, :, 1, g], qkv[:, :, 2, g]
        o, lse = banded_causal_attn(to_streams(q, dil), to_streams(k, dil), to_streams(v, dil), win // dil)
        outs.append(from_streams(o, B_, dil))
        lses.append(from_streams(lse, B_, dil))
        keep = min(win, S_)
        rows.append(jnp.stack([k[:, S_ - keep:], v[:, S_ - keep:]], axis=2))
    o = merge_groups(outs, lses)
    return o.reshape(B_, S_, A_INNER) @ w_out, rows


def dilated_sample(h, bufs, w_in, w_out):
    DB, T, _ = h.shape
    qkv = (h @ w_in).reshape(DB, T, 3, A_GROUPS, A_HEADS, HEAD_DIM)
    outs, lses, rows = [], [], []
    for g, (win, dil) in enumerate(zip(A_WINDOWS, A_DILATIONS)):
        q, k, v = qkv[:, :, 0, g], qkv[:, :, 1, g], qkv[:, :, 2, g]
        new = jnp.stack([k, v], axis=2)
        buf = bufs[g]
        Lb = buf.shape[1]
        kv_all = jnp.concatenate([buf, new.astype(buf.dtype)], axis=1)
        steps = win // dil
        idx = Lb + jnp.arange(T)[:, None] - dil * jnp.arange(steps + 1)[None, :]
        sel = kv_all[:, jnp.maximum(idx, 0)]
        s = jnp.einsum('bthd,btnhd->bthn', q, sel[:, :, :, 0]).astype(F32) * ATTN_SCALE
        p, lse = masked_softmax(s, (idx >= 0)[None, :, None, :])
        outs.append(jnp.einsum('bthn,btnhd->bthd', p.astype(sel.dtype), sel[:, :, :, 1]))
        lses.append(lse)
        rows.append(new)
    o = merge_groups(outs, lses)
    return o.reshape(DB, T, A_INNER) @ w_out, rows


def spatial_gating(h, w_in, ln_g, ln_b, w_s, b_s, w_out):
    B_, S_, _ = h.shape
    z = jax.nn.gelu(h @ w_in)
    u = z[..., :B_WIDTH]
    v = layernorm(z[..., B_WIDTH:], ln_g, ln_b)
    n = min(S_, CHUNK)
    vc = v.reshape(B_, S_ // n, n, B_GROUPS, B_GROUP_DIM)
    wm = jnp.tril(w_s[:, :n, :n])
    s = jnp.einsum('gij,bcjgk->bcigk', wm, vc) + b_s[:, :n].T[:, :, None].astype(vc.dtype)
    return (u * s.reshape(B_, S_, B_WIDTH)) @ w_out, v


def stick_breaking(q, k, v, qpos, kpos):
    z = jnp.einsum('bqhd,bkhd->bhqk', q, k).astype(F32) * ATTN_SCALE
    valid = kpos[None, :] < qpos[:, None]
    log_keep = jnp.where(valid, jax.nn.log_sigmoid(-z), 0.0)
    after = lax.cumsum(log_keep, axis=3, reverse=True) - log_keep
    a = jnp.where(valid, jnp.exp(jax.nn.log_sigmoid(z) + after), 0.0)
    return jnp.einsum('bhqk,bkhd->bqhd', a.astype(v.dtype), v)


def sb_prompt(h, w_in, b_in, w_out):
    B_, S_, _ = h.shape
    qkv = (h @ w_in + b_in).reshape(B_, S_, 3, C_HEADS, HEAD_DIM)
    q, k, v = qkv[:, :, 0], qkv[:, :, 1], qkv[:, :, 2]
    nb = S_ // BLK
    qb = q.reshape(B_, nb, BLK, C_HEADS, HEAD_DIM).swapaxes(0, 1)
    kpos = jnp.arange(S_)
    o = lax.map(lambda a: stick_breaking(a[0], k, v, a[1] + jnp.arange(BLK), kpos),
                (qb, jnp.arange(nb) * BLK))
    o = o.swapaxes(0, 1).reshape(B_, S_, C_INNER)
    return o @ w_out, jnp.stack([k, v], axis=2)


def sb_sample(h, pool, page_table, w_in, b_in, w_out):
    DB, T, _ = h.shape
    qkv = (h @ w_in + b_in).reshape(DB, T, 3, C_HEADS, HEAD_DIM)
    q, k, v = qkv[:, :, 0], qkv[:, :, 1], qkv[:, :, 2]
    new = jnp.stack([k, v], axis=2)
    past = pool[page_table]
    past = past.reshape((DB, -1) + past.shape[3:])
    P = past.shape[1]
    kv_all = jnp.concatenate([past, new.astype(past.dtype)], axis=1)
    o = stick_breaking(q, kv_all[:, :, 0], kv_all[:, :, 1], P + jnp.arange(T), jnp.arange(P + T))
    return o.reshape(DB, T, C_INNER) @ w_out, new


def block_tail(x, p, g_ffn, w1, w2, g_ple, w_proj, w_gate):
    h = rmsnorm(x, g_ffn)
    x = x + jnp.square(jax.nn.relu(h @ w1)) @ w2
    gate = jax.nn.sigmoid(rmsnorm(x, g_ple) @ w_gate)
    return x + (p @ w_proj) * gate


def setup_inputs(seed: int = 0) -> dict:
    key = jax.random.key(seed)
    ks = iter(jax.random.split(key, 48))

    def nrm(shape, scale=1.0):
        return jax.random.normal(next(ks), shape, F32) * scale

    def gain(shape):
        return 1.0 + nrm(shape, 0.05)

    n_pages = PAST_LEN // PAGE_SIZE
    n_used = DEC_BATCH * n_pages
    n_pool = n_used + max(1, n_used // 4)
    page_table = jax.random.permutation(next(ks), n_pool)[:n_used].reshape(DEC_BATCH, n_pages).astype(jnp.int32)
    d = D_MODEL
    sb_dir = nrm((N_C_LAYERS, C_HEADS, HEAD_DIM))
    sb_dir = sb_dir / jnp.linalg.norm(sb_dir, axis=-1, keepdims=True)
    b_q = SB_QK_OFFSET * sb_dir + nrm((N_C_LAYERS, C_HEADS, HEAD_DIM), 0.02)
    b_k = -SB_QK_OFFSET * sb_dir + nrm((N_C_LAYERS, C_HEADS, HEAD_DIM), 0.02)
    b_v = nrm((N_C_LAYERS, C_HEADS, HEAD_DIM), 0.02)
    c_b_in = jnp.stack([b_q, b_k, b_v], axis=1).reshape(N_C_LAYERS, 3 * C_INNER)
    kv_offset = jnp.stack([b_k, jnp.zeros_like(b_k)], axis=1)
    cache_c_kv = nrm((N_C_LAYERS, n_pool, PAGE_SIZE, 2, C_HEADS, HEAD_DIM)) + kv_offset[:, None, None]
    return {
        'x_prompt': nrm((BATCH, SEQ, d)),
        'x_sample': nrm((DEC_BATCH, DEC_SEQ, d)),
        'p_prompt': nrm((DEPTH, BATCH, SEQ, PLE_DIM)),
        'p_sample': nrm((DEPTH, DEC_BATCH, DEC_SEQ, PLE_DIM)),
        'cache_a_w128': nrm((N_A_LAYERS, DEC_BATCH, min(A_WINDOWS[0], PAST_LEN), 2, A_HEADS, HEAD_DIM)),
        'cache_a_w512': nrm((N_A_LAYERS, DEC_BATCH, min(A_WINDOWS[1], PAST_LEN), 2, A_HEADS, HEAD_DIM)),
        'cache_a_w2048': nrm((N_A_LAYERS, DEC_BATCH, min(A_WINDOWS[2], PAST_LEN), 2, A_HEADS, HEAD_DIM)),
        'cache_c_kv': cache_c_kv,
        'page_table': page_table,
        'norm_mix': gain((DEPTH, d)),
        'norm_ffn': gain((DEPTH, d)),
        'norm_ple': gain((DEPTH, d)),
        'norm_final': gain((d,)),
        'a_w_in': nrm((N_A_LAYERS, d, 3 * A_GROUPS * A_INNER), d ** -0.5),
        'a_w_out': nrm((N_A_LAYERS, A_INNER, d), A_INNER ** -0.5),
        'b_w_in': nrm((N_B_LAYERS, d, 2 * B_WIDTH), d ** -0.5),
        'b_ln_g': gain((N_B_LAYERS, B_WIDTH)),
        'b_ln_b': nrm((N_B_LAYERS, B_WIDTH), 0.02),
        'b_w_s': nrm((N_B_LAYERS, B_GROUPS, CHUNK, CHUNK), CHUNK ** -0.5),
        'b_b_s': 1.0 + nrm((N_B_LAYERS, B_GROUPS, CHUNK), 0.1),
        'b_w_out': nrm((N_B_LAYERS, B_WIDTH, d), B_WIDTH ** -0.5),
        'c_w_in': nrm((N_C_LAYERS, d, 3 * C_INNER), d ** -0.5),
        'c_b_in': c_b_in,
        'c_w_out': nrm((N_C_LAYERS, C_INNER, d), C_INNER ** -0.5),
        'ffn_w1': nrm((DEPTH, d, D_FF), d ** -0.5),
        'ffn_w2': nrm((DEPTH, D_FF, d), D_FF ** -0.5),
        'ple_w_proj': nrm((DEPTH, PLE_DIM, d), PLE_DIM ** -0.5),
        'ple_w_gate': nrm((DEPTH, d, d), d ** -0.5),
    }


def reference(x_prompt, x_sample, p_prompt, p_sample, cache_a_w128, cache_a_w512, cache_a_w2048,
              cache_c_kv, page_table, norm_mix, norm_ffn, norm_ple, norm_final, a_w_in, a_w_out,
              b_w_in, b_ln_g, b_ln_b, b_w_s, b_b_s, b_w_out, c_w_in, c_b_in, c_w_out, ffn_w1, ffn_w2,
              ple_w_proj, ple_w_gate):
    xp, xs = x_prompt, x_sample
    a_p = [[] for _ in A_WINDOWS]
    a_s = [[] for _ in A_WINDOWS]
    b_s_rows, c_p, c_s = [], [], []
    for i in range(DEPTH):
        kind, j = i % N_MIXERS, i // N_MIXERS
        hp, hs = rmsnorm(xp, norm_mix[i]), rmsnorm(xs, norm_mix[i])
        if kind == 0:
            yp, rp = dilated_prompt(hp, a_w_in[j], a_w_out[j])
            ys, rs = dilated_sample(hs, (cache_a_w128[j], cache_a_w512[j], cache_a_w2048[j]), a_w_in[j], a_w_out[j])
            for g in range(A_GROUPS):
                a_p[g].append(rp[g])
                a_s[g].append(rs[g])
        elif kind == 1:
            yp, _ = spatial_gating(hp, b_w_in[j], b_ln_g[j], b_ln_b[j], b_w_s[j], b_b_s[j], b_w_out[j])
            ys, vs = spatial_gating(hs, b_w_in[j], b_ln_g[j], b_ln_b[j], b_w_s[j], b_b_s[j], b_w_out[j])
            b_s_rows.append(vs)
        else:
            yp, kvp = sb_prompt(hp, c_w_in[j], c_b_in[j], c_w_out[j])
            ys, kvs = sb_sample(hs, cache_c_kv[j], page_table, c_w_in[j], c_b_in[j], c_w_out[j])
            c_p.append(kvp)
            c_s.append(kvs)
        xp = block_tail(xp + yp, p_prompt[i], norm_ffn[i], ffn_w1[i], ffn_w2[i], norm_ple[i], ple_w_proj[i], ple_w_gate[i])
        xs = block_tail(xs + ys, p_sample[i], norm_ffn[i], ffn_w1[i], ffn_w2[i], norm_ple[i], ple_w_proj[i], ple_w_gate[i])
    y_prompt = rmsnorm(xp, norm_final)
    y_sample = rmsnorm(xs, norm_final)
    new_a_w128_prompt = jnp.stack(a_p[0])
    new_a_w512_prompt = jnp.stack(a_p[1])
    new_a_w2048_prompt = jnp.stack(a_p[2])
    new_a_w128_sample = jnp.stack(a_s[0])
    new_a_w512_sample = jnp.stack(a_s[1])
    new_a_w2048_sample = jnp.stack(a_s[2])
    new_b_v_sample = jnp.stack(b_s_rows)
    new_c_kv_prompt = jnp.stack(c_p)
    new_c_kv_sample = jnp.stack(c_s)
    return (y_prompt, y_sample, new_a_w128_prompt, new_a_w512_prompt, new_a_w2048_prompt,
            new_a_w128_sample, new_a_w512_sample, new_a_w2048_sample, new_b_v_sample,
            new_c_kv_prompt, new_c_kv_sample)
```

```python
import functools
import math

import jax
import jax.numpy as jnp
from jax import lax
from jax.experimental import pallas as pl
from jax.experimental.pallas import tpu as pltpu

F32 = jnp.float32
BF16 = jnp.bfloat16

HEAD_DIM = 64
HEADS = 16
INNER = HEADS * HEAD_DIM
ATTN_SCALE = 1.0 / math.sqrt(HEAD_DIM)
BLK = 128
A_WINDOWS = (128, 512, 2048)
A_DILATIONS = (1, 4, 16)
A_GROUPS = 3
B_GROUPS = 8
CHUNK = 128
SB_QBLK = 256
N_MIXERS = 3
EPS = 1e-6
LANES = 128
VMEM_LIMIT = 56 * 1024 * 1024


def _params(*sem):
    return pltpu.CompilerParams(dimension_semantics=sem, vmem_limit_bytes=VMEM_LIMIT)


def _rms(xf, g):
    return xf * lax.rsqrt(jnp.mean(xf * xf, axis=-1, keepdims=True) + EPS) * g


def _dot(a, b):
    return jnp.dot(a, b, preferred_element_type=F32)


def _dot_nt(a, b):
    return lax.dot_general(a, b, (((1,), (1,)), ((), ())), preferred_element_type=F32)


def _dot_split(a, b_bf16):
    hi = a.astype(BF16)
    lo = (a - hi.astype(F32)).astype(BF16)
    return _dot(hi, b_bf16) + _dot(lo, b_bf16)


def _softplus(z):
    return jnp.maximum(z, 0.0) + jnp.log1p(jnp.exp(-jnp.abs(z)))


def _head_rows_mask(rows):
    head_of_lane = lax.broadcasted_iota(jnp.int32, (rows, INNER), 1) // HEAD_DIM
    return head_of_lane == lax.broadcasted_iota(jnp.int32, (rows, INNER), 0)


def _norm_matmul_kernel(x_ref, g_ref, w_ref, *rest, has_bias):
    if has_bias:
        b_ref, o_ref, h_ref = rest
    else:
        o_ref, h_ref = rest

    @pl.when(pl.program_id(1) == 0)
    def _():
        h_ref[...] = _rms(x_ref[...], g_ref[...]).astype(BF16)

    acc = _dot(h_ref[...], w_ref[...])
    if has_bias:
        acc = acc + b_ref[...]
    o_ref[...] = acc.astype(o_ref.dtype)


def norm_matmul(x, g, w, bias, out_dtype, tm, tn):
    M, D = x.shape
    N = w.shape[1]
    in_specs = [pl.BlockSpec((tm, D), lambda i, j: (i, 0)),
                pl.BlockSpec((1, D), lambda i, j: (0, 0)),
                pl.BlockSpec((D, tn), lambda i, j: (0, j))]
    args = [x, g.reshape(1, D), w]
    if bias is not None:
        in_specs.append(pl.BlockSpec((1, tn), lambda i, j: (0, j)))
        args.append(bias.reshape(1, N))
    return pl.pallas_call(
        functools.partial(_norm_matmul_kernel, has_bias=bias is not None),
        grid=(M // tm, N // tn),
        in_specs=in_specs,
        out_specs=pl.BlockSpec((tm, tn), lambda i, j: (i, j)),
        out_shape=jax.ShapeDtypeStruct((M, N), out_dtype),
        scratch_shapes=[pltpu.VMEM((tm, D), BF16)],
        compiler_params=_params("arbitrary", "arbitrary"),
        name="norm_matmul",
    )(*args)


def _outproj_kernel(x_ref, o_ref, w_ref, y_ref):
    y_ref[...] = x_ref[...] + _dot(o_ref[...].astype(BF16), w_ref[...])


def outproj(x, o, w, tm):
    M, D = x.shape
    K = o.shape[1]
    return pl.pallas_call(
        _outproj_kernel,
        grid=(M // tm,),
        in_specs=[pl.BlockSpec((tm, D), lambda i: (i, 0)),
                  pl.BlockSpec((tm, K), lambda i: (i, 0)),
                  pl.BlockSpec((K, D), lambda i: (0, 0))],
        out_specs=pl.BlockSpec((tm, D), lambda i: (i, 0)),
        out_shape=jax.ShapeDtypeStruct((M, D), F32),
        compiler_params=_params("arbitrary"),
        name="outproj",
    )(x, o, w)


def _merge_outproj_kernel(x_ref, o0_ref, o1_ref, o2_ref, l0_ref, l1_ref, l2_ref, w_ref, y_ref):
    l0, l1, l2 = l0_ref[...], l1_ref[...], l2_ref[...]
    m = jnp.maximum(jnp.maximum(l0, l1), l2)
    e0, e1, e2 = jnp.exp(l0 - m), jnp.exp(l1 - m), jnp.exp(l2 - m)
    den = e0 + e1 + e2
    expand = (lax.broadcasted_iota(jnp.int32, (LANES, INNER), 1) // HEAD_DIM
              == lax.broadcasted_iota(jnp.int32, (LANES, INNER), 0)).astype(BF16)
    o = (_dot_split(e0 / den, expand) * o0_ref[...].astype(F32)
         + _dot_split(e1 / den, expand) * o1_ref[...].astype(F32)
         + _dot_split(e2 / den, expand) * o2_ref[...].astype(F32))
    y_ref[...] = x_ref[...] + _dot(o.astype(BF16), w_ref[...])


def merge_outproj(x, outs, lses, w, tm):
    M, D = x.shape
    row = lambda width: pl.BlockSpec((tm, width), lambda i: (i, 0))
    return pl.pallas_call(
        _merge_outproj_kernel,
        grid=(M // tm,),
        in_specs=[row(D), row(INNER), row(INNER), row(INNER), row(LANES), row(LANES), row(LANES),
                  pl.BlockSpec((INNER, D), lambda i: (0, 0))],
        out_specs=row(D),
        out_shape=jax.ShapeDtypeStruct((M, D), F32),
        compiler_params=_params("arbitrary"),
        name="merge_outproj",
    )(x, *outs, *lses, w)


def _dil_attn_kernel(q_ref, kp_ref, kc_ref, vp_ref, vc_ref, o_ref, lse_ref):
    row = lax.broadcasted_iota(jnp.int32, (BLK, BLK), 0)
    col = lax.broadcasted_iota(jnp.int32, (BLK, BLK), 1)
    mask_c = col <= row
    mask_p = (col >= row) & (pl.program_id(2) > 0)
    lane = lax.broadcasted_iota(jnp.int32, (BLK, LANES), 1)
    first_head = lane < HEAD_DIM
    lse_tile = jnp.zeros((BLK, LANES), F32)
    for pair in range(HEADS // 2):
        sl = slice(pair * LANES, (pair + 1) * LANES)
        q2 = q_ref[:, sl] * ATTN_SCALE
        kp, kc, vp, vc = kp_ref[:, sl], kc_ref[:, sl], vp_ref[:, sl], vc_ref[:, sl]
        outs = []
        for hh in range(2):
            qh = jnp.where(first_head if hh == 0 else ~first_head, q2, jnp.zeros_like(q2))
            s_c = jnp.where(mask_c, _dot_nt(qh, kc), -jnp.inf)
            s_p = jnp.where(mask_p, _dot_nt(qh, kp), -jnp.inf)
            m = jnp.maximum(jnp.max(s_c, axis=1, keepdims=True), jnp.max(s_p, axis=1, keepdims=True))
            e_c = jnp.exp(s_c - m)
            e_p = jnp.exp(s_p - m)
            den = jnp.sum(e_c, axis=1, keepdims=True) + jnp.sum(e_p, axis=1, keepdims=True)
            pv = _dot(e_c.astype(BF16), vc) + _dot(e_p.astype(BF16), vp)
            outs.append(pv / den)
            lse_tile = jnp.where(lane == 2 * pair + hh, m + jnp.log(den), lse_tile)
        o_ref[:, sl] = jnp.where(first_head, outs[0], outs[1]).astype(o_ref.dtype)
    lse_ref[...] = lse_tile


def dil_attn(qkv, g, dil):
    B, S, W = qkv.shape
    L = S // dil
    nblk = L // BLK
    ncol = W // INNER
    view = qkv.reshape(B, L, dil * W)
    cur = lambda part: pl.BlockSpec((None, BLK, INNER),
                                    lambda b, r, t: (b, t, r * ncol + part * A_GROUPS + g))
    prev = lambda part: pl.BlockSpec((None, BLK, INNER),
                                     lambda b, r, t: (b, jnp.maximum(t - 1, 0), r * ncol + part * A_GROUPS + g))
    o, lse = pl.pallas_call(
        _dil_attn_kernel,
        grid=(B, dil, nblk),
        in_specs=[cur(0), prev(1), cur(1), prev(2), cur(2)],
        out_specs=[pl.BlockSpec((None, BLK, INNER), lambda b, r, t: (b, t, r)),
                   pl.BlockSpec((None, BLK, LANES), lambda b, r, t: (b, t, r))],
        out_shape=[jax.ShapeDtypeStruct((B, L, dil * INNER), BF16),
                   jax.ShapeDtypeStruct((B, L, dil * LANES), F32)],
        compiler_params=_params("arbitrary", "arbitrary", "arbitrary"),
        name=f"dil_attn_d{dil}",
    )(view, view, view, view, view)
    return o.reshape(B * S, INNER), lse.reshape(B * S, LANES)


def _dil_sample_kernel(qkv_ref, c0_ref, c1_ref, c2_ref, o_ref):
    head = _head_rows_mask(HEADS)
    scores, new_scores = [], []
    for g, c_ref in enumerate((c0_ref, c1_ref, c2_ref)):
        q = qkv_ref[:, g * INNER:(g + 1) * INNER]
        k_new = qkv_ref[:, (A_GROUPS + g) * INNER:(A_GROUPS + g + 1) * INNER]
        q_heads = jnp.where(head, q, 0.0)
        scores.append(_dot_nt(q_heads.astype(BF16), c_ref[:, :INNER].astype(BF16)) * ATTN_SCALE)
        new_scores.append(jnp.sum(q_heads * k_new, axis=1, keepdims=True) * ATTN_SCALE)
    m = new_scores[0]
    for g in range(A_GROUPS):
        m = jnp.maximum(m, jnp.maximum(new_scores[g], jnp.max(scores[g], axis=1, keepdims=True)))
    num = jnp.zeros((HEADS, INNER), F32)
    den = jnp.zeros((HEADS, 1), F32)
    for g, c_ref in enumerate((c0_ref, c1_ref, c2_ref)):
        v_new = qkv_ref[:, (2 * A_GROUPS + g) * INNER:(2 * A_GROUPS + g + 1) * INNER]
        e = jnp.exp(scores[g] - m)
        e_new = jnp.exp(new_scores[g] - m)
        num = num + _dot(e.astype(BF16), c_ref[:, INNER:].astype(BF16)) + e_new * v_new
        den = den + jnp.sum(e, axis=1, keepdims=True) + e_new
    o_ref[...] = jnp.sum(jnp.where(head, num / den, 0.0), axis=0, keepdims=True)


def dil_sample(qkv, caches):
    DB, W = qkv.shape
    views = []
    for cache, win, dil in zip(caches, A_WINDOWS, A_DILATIONS):
        assert cache.shape[1] == win and win // dil == BLK
        views.append(cache.reshape(DB, BLK, dil * 2 * INNER))
    o = pl.pallas_call(
        _dil_sample_kernel,
        grid=(DB,),
        in_specs=[pl.BlockSpec((None, 1, W), lambda b: (b, 0, 0))]
        + [pl.BlockSpec((None, BLK, 2 * INNER), lambda b: (b, 0, 0))] * A_GROUPS,
        out_specs=pl.BlockSpec((None, 1, INNER), lambda b: (b, 0, 0)),
        out_shape=jax.ShapeDtypeStruct((DB, 1, INNER), F32),
        compiler_params=_params("arbitrary"),
        name="dil_sample",
    )(qkv.reshape(DB, 1, W), *views)
    return o.reshape(DB, INNER)


def _sgu_front(x, g_ref, win_ref, lng_ref, lnb_ref):
    z = jax.nn.gelu(_dot(_rms(x, g_ref[...]).astype(BF16), win_ref[...]))
    width = z.shape[1] // 2
    u, vr = z[:, :width], z[:, width:]
    mu = jnp.mean(vr, axis=-1, keepdims=True)
    var = jnp.mean(jnp.square(vr - mu), axis=-1, keepdims=True)
    return u, (vr - mu) * lax.rsqrt(var + EPS) * lng_ref[...] + lnb_ref[...]


def _sgu_prompt_kernel(x_ref, g_ref, win_ref, lng_ref, lnb_ref, ws_ref, bs_ref, wout_ref, y_ref, us_ref):
    x = x_ref[...]
    u, v = _sgu_front(x, g_ref, win_ref, lng_ref, lnb_ref)
    causal = (lax.broadcasted_iota(jnp.int32, (CHUNK, CHUNK), 0)
              >= lax.broadcasted_iota(jnp.int32, (CHUNK, CHUNK), 1))
    for grp in range(B_GROUPS):
        cols = slice(grp * CHUNK, (grp + 1) * CHUNK)
        wm = jnp.where(causal, ws_ref[grp], 0.0).astype(BF16)
        for c in range(x.shape[0] // CHUNK):
            rows = slice(c * CHUNK, (c + 1) * CHUNK)
            s = _dot(wm, v[rows, cols].astype(BF16)) + bs_ref[:, cols]
            us_ref[rows, cols] = (u[rows, cols] * s).astype(BF16)
    y_ref[...] = x + _dot(us_ref[...], wout_ref[...])


def sgu_prompt(x, g, w_in, ln_g, ln_b, w_s, b_s, w_out, tm):
    M, D = x.shape
    width = w_out.shape[0]
    bias = jnp.repeat(b_s.T, width // B_GROUPS, axis=1)
    const = lambda shape: pl.BlockSpec(shape, lambda i: (0,) * len(shape))
    return pl.pallas_call(
        _sgu_prompt_kernel,
        grid=(M // tm,),
        in_specs=[pl.BlockSpec((tm, D), lambda i: (i, 0)), const((1, D)), const((D, 2 * width)),
                  const((1, width)), const((1, width)), const((B_GROUPS, CHUNK, CHUNK)),
                  const((CHUNK, width)), const((width, D))],
        out_specs=pl.BlockSpec((tm, D), lambda i: (i, 0)),
        out_shape=jax.ShapeDtypeStruct((M, D), F32),
        scratch_shapes=[pltpu.VMEM((tm, width), BF16)],
        compiler_params=_params("arbitrary"),
        name="sgu_prompt",
    )(x, g.reshape(1, D), w_in, ln_g.reshape(1, width), ln_b.reshape(1, width), w_s, bias, w_out)


def _sgu_sample_kernel(x_ref, g_ref, win_ref, lng_ref, lnb_ref, sw_ref, sb_ref, wout_ref, y_ref, v_ref):
    x = x_ref[...]
    u, v = _sgu_front(x, g_ref, win_ref, lng_ref, lnb_ref)
    s = sw_ref[...] * v + sb_ref[...]
    y_ref[...] = x + _dot((u * s).astype(BF16), wout_ref[...])
    v_ref[...] = v


def sgu_sample(x, g, w_in, ln_g, ln_b, w_s, b_s, w_out):
    M, D = x.shape
    width = w_out.shape[0]
    sw = jnp.repeat(w_s[:, 0, 0], width // B_GROUPS).reshape(1, width)
    sb = jnp.repeat(b_s[:, 0], width // B_GROUPS).reshape(1, width)
    return pl.pallas_call(
        _sgu_sample_kernel,
        out_shape=[jax.ShapeDtypeStruct((M, D), F32), jax.ShapeDtypeStruct((M, width), F32)],
        compiler_params=pltpu.CompilerParams(vmem_limit_bytes=VMEM_LIMIT),
        name="sgu_sample",
    )(x, g.reshape(1, D), w_in, ln_g.reshape(1, width), ln_b.reshape(1, width), sw, sb, w_out)


def _sb_attn_kernel(q_ref, k_ref, v_ref, o_ref):
    i = pl.program_id(2)
    n = SB_QBLK
    lane = lax.broadcasted_iota(jnp.int32, (n, LANES), 1)
    first_head = lane < HEAD_DIM
    row = lax.broadcasted_iota(jnp.int32, (n, n), 0)
    col = lax.broadcasted_iota(jnp.int32, (n, n), 1)
    later = (row > col).astype(BF16)
    valid = col < row
    q2 = q_ref[...] * ATTN_SCALE
    outs = []
    for hh in range(2):
        qh = jnp.where(first_head if hh == 0 else ~first_head, q2, jnp.zeros_like(q2))

        def block(kb, o, c, diag):
            start = pl.multiple_of(kb * n, n)
            z = _dot_nt(qh, k_ref[pl.ds(start, n), :])
            sp = _softplus(z)
            lk = jnp.where(valid, -sp, 0.0) if diag else -sp
            a = jnp.exp(z - sp + _dot_split(lk, later) + c)
            if diag:
                a = jnp.where(valid, a, 0.0)
            o = o + _dot(a.astype(BF16), v_ref[pl.ds(start, n), :])
            return o, c + jnp.sum(lk, axis=1, keepdims=True)

        o, c = block(i, jnp.zeros((n, LANES), F32), jnp.zeros((n, 1), F32), True)
        o, c = lax.fori_loop(0, i, lambda s, oc: block(i - 1 - s, oc[0], oc[1], False), (o, c))
        outs.append(o)
    o_ref[...] = jnp.where(first_head, outs[0], outs[1]).astype(o_ref.dtype)


def sb_attn(qkv):
    B, S, _ = qkv.shape
    pairs = INNER // LANES
    return pl.pallas_call(
        _sb_attn_kernel,
        grid=(B, pairs, S // SB_QBLK),
        in_specs=[pl.BlockSpec((None, SB_QBLK, LANES), lambda b, p, i: (b, i, p)),
                  pl.BlockSpec((None, S, LANES), lambda b, p, i: (b, 0, pairs + p)),
                  pl.BlockSpec((None, S, LANES), lambda b, p, i: (b, 0, 2 * pairs + p))],
        out_specs=pl.BlockSpec((None, SB_QBLK, LANES), lambda b, p, i: (b, i, p)),
        out_shape=jax.ShapeDtypeStruct((B, S, INNER), BF16),
        compiler_params=_params("arbitrary", "arbitrary", "arbitrary"),
        name="sb_attn",
    )(qkv, qkv, qkv)


def _sb_sample_kernel(pt_ref, q_ref, kv_ref, o_ref, acc_ref, c_ref):
    del pt_ref
    p = pl.program_id(1)

    @pl.when(p == 0)
    def _():
        acc_ref[...] = jnp.zeros_like(acc_ref)
        c_ref[...] = jnp.zeros_like(c_ref)

    n = kv_ref.shape[0]
    head = _head_rows_mask(HEADS)
    q_heads = jnp.where(head, q_ref[...], 0.0).astype(BF16)
    z = _dot_nt(q_heads, kv_ref[:, :INNER].astype(BF16)) * ATTN_SCALE
    sp = _softplus(z)
    lk = -sp
    later = (lax.broadcasted_iota(jnp.int32, (n, n), 0)
             > lax.broadcasted_iota(jnp.int32, (n, n), 1)).astype(BF16)
    a = jnp.exp(z - sp + _dot_split(lk, later) + c_ref[...])
    acc_ref[...] += _dot(a.astype(BF16), kv_ref[:, INNER:].astype(BF16))
    c_ref[...] += jnp.sum(lk, axis=1, keepdims=True)

    @pl.when(p == pl.num_programs(1) - 1)
    def _():
        o_ref[...] = jnp.sum(jnp.where(head, acc_ref[...], 0.0), axis=0, keepdims=True)


def sb_sample(q, pool, page_table):
    DB = q.shape[0]
    n_pool, page = pool.shape[:2]
    n_pages = page_table.shape[1]
    o = pl.pallas_call(
        _sb_sample_kernel,
        grid_spec=pltpu.PrefetchScalarGridSpec(
            num_scalar_prefetch=1,
            grid=(DB, n_pages),
            in_specs=[pl.BlockSpec((None, 1, INNER), lambda b, p, pt: (b, 0, 0)),
                      pl.BlockSpec((None, page, 2 * INNER),
                                   lambda b, p, pt: (pt[b, n_pages - 1 - p], 0, 0))],
            out_specs=pl.BlockSpec((None, 1, INNER), lambda b, p, pt: (b, 0, 0)),
            scratch_shapes=[pltpu.VMEM((HEADS, INNER), F32), pltpu.VMEM((HEADS, 1), F32)]),
        out_shape=jax.ShapeDtypeStruct((DB, 1, INNER), F32),
        compiler_params=_params("arbitrary", "arbitrary"),
        name="sb_sample",
    )(page_table, q.reshape(DB, 1, INNER), pool.reshape(n_pool, page, 2 * INNER))
    return o.reshape(DB, INNER)


def _ffn_ple_kernel(x_ref, p_ref, gf_ref, w1_ref, w2_ref, gp_ref, wproj_ref, wgate_ref, *rest, final):
    if final:
        gfin_ref, y_ref, h_ref, acc_ref = rest
    else:
        y_ref, h_ref, acc_ref = rest
    f = pl.program_id(1)

    @pl.when(f == 0)
    def _():
        h_ref[...] = _rms(x_ref[...], gf_ref[...]).astype(BF16)
        acc_ref[...] = jnp.zeros_like(acc_ref)

    a = jnp.square(jnp.maximum(_dot(h_ref[...], w1_ref[...]), 0.0))
    acc_ref[...] += _dot(a.astype(BF16), w2_ref[...])

    @pl.when(f == pl.num_programs(1) - 1)
    def _():
        x2 = x_ref[...] + acc_ref[...]
        gate = jax.nn.sigmoid(_dot(_rms(x2, gp_ref[...]).astype(BF16), wgate_ref[...]))
        y = x2 + _dot(p_ref[...].astype(BF16), wproj_ref[...]) * gate
        if final:
            y = _rms(y, gfin_ref[...])
        y_ref[...] = y


def ffn_ple(x, p, g_ffn, w1, w2, g_ple, w_proj, w_gate, g_final, tm, tf):
    M, D = x.shape
    F = w1.shape[1]
    P = p.shape[1]
    const = lambda shape: pl.BlockSpec(shape, lambda i, f: (0, 0))
    in_specs = [pl.BlockSpec((tm, D), lambda i, f: (i, 0)),
                pl.BlockSpec((tm, P), lambda i, f: (i, 0)),
                const((1, D)),
                pl.BlockSpec((D, tf), lambda i, f: (0, f)),
                pl.BlockSpec((tf, D), lambda i, f: (f, 0)),
                const((1, D)), const((P, D)), const((D, D))]
    args = [x, p, g_ffn.reshape(1, D), w1, w2, g_ple.reshape(1, D), w_proj, w_gate]
    if g_final is not None:
        in_specs.append(const((1, D)))
        args.append(g_final.reshape(1, D))
    return pl.pallas_call(
        functools.partial(_ffn_ple_kernel, final=g_final is not None),
        grid=(M // tm, F // tf),
        in_specs=in_specs,
        out_specs=pl.BlockSpec((tm, D), lambda i, f: (i, 0)),
        out_shape=jax.ShapeDtypeStruct((M, D), F32),
        scratch_shapes=[pltpu.VMEM((tm, D), BF16), pltpu.VMEM((tm, D), F32)],
        compiler_params=_params("arbitrary", "arbitrary"),
        name="ffn_ple",
    )(*args)


def kernel(x_prompt, x_sample, p_prompt, p_sample, cache_a_w128, cache_a_w512, cache_a_w2048, cache_c_kv, page_table, norm_mix, norm_ffn, norm_ple, norm_final, a_w_in, a_w_out, b_w_in, b_ln_g, b_ln_b, b_w_s, b_b_s, b_w_out, c_w_in, c_b_in, c_w_out, ffn_w1, ffn_w2, ple_w_proj, ple_w_gate):
    B, S, D = x_prompt.shape
    DB, T, _ = x_sample.shape
    assert T == 1, "the sample path handles one new token per sequence"
    depth = norm_mix.shape[0]
    MP, MS = B * S, DB * T
    tm = min(512, MP)
    bf = lambda w: w.astype(BF16)

    xp = x_prompt.reshape(MP, D)
    xs = x_sample.reshape(MS, D)
    caches_a = (cache_a_w128, cache_a_w512, cache_a_w2048)
    a_p = [[] for _ in A_WINDOWS]
    a_s = [[] for _ in A_WINDOWS]
    b_s_rows, c_p, c_s = [], [], []

    for i in range(depth):
        kind, j = i % N_MIXERS, i // N_MIXERS
        if kind == 0:
            w_in, w_out = bf(a_w_in[j]), bf(a_w_out[j])
            qkv_p = norm_matmul(xp, norm_mix[i], w_in, None, BF16, tm, 1024)
            qkv_s = norm_matmul(xs, norm_mix[i], w_in, None, F32, MS, 1024)
            qkv_p3 = qkv_p.reshape(B, S, -1)
            outs, lses = zip(*[dil_attn(qkv_p3, g, dil) for g, dil in enumerate(A_DILATIONS)])
            xp = merge_outproj(xp, outs, lses, w_out, tm)
            xs = outproj(xs, dil_sample(qkv_s, [c[j] for c in caches_a]), w_out, MS)
            kv_p = qkv_p3.reshape(B, S, 3, A_GROUPS, HEADS, HEAD_DIM).astype(F32)
            kv_s = qkv_s.reshape(DB, T, 3, A_GROUPS, HEADS, HEAD_DIM)
            for g, win in enumerate(A_WINDOWS):
                keep = min(win, S)
                a_p[g].append(jnp.stack([kv_p[:, S - keep:, 1, g], kv_p[:, S - keep:, 2, g]], axis=2))
                a_s[g].append(jnp.stack([kv_s[:, :, 1, g], kv_s[:, :, 2, g]], axis=2))
        elif kind == 1:
            w_in, w_out = bf(b_w_in[j]), bf(b_w_out[j])
            xp = sgu_prompt(xp, norm_mix[i], w_in, b_ln_g[j], b_ln_b[j], b_w_s[j], b_b_s[j], w_out, min(256, MP))
            xs, v_s = sgu_sample(xs, norm_mix[i], w_in, b_ln_g[j], b_ln_b[j], b_w_s[j], b_b_s[j], w_out)
            b_s_rows.append(v_s.reshape(DB, T, -1))
        else:
            w_in, w_out = bf(c_w_in[j]), bf(c_w_out[j])
            qkv_p = norm_matmul(xp, norm_mix[i], w_in, c_b_in[j], BF16, tm, 1024)
            qkv_s = norm_matmul(xs, norm_mix[i], w_in, c_b_in[j], F32, MS, 1024)
            o_p = sb_attn(qkv_p.reshape(B, S, -1))
            xp = outproj(xp, o_p.reshape(MP, INNER), w_out, tm)
            xs = outproj(xs, sb_sample(qkv_s[:, :INNER], cache_c_kv[j], page_table), w_out, MS)
            c_p.append(qkv_p[:, INNER:].astype(F32).reshape(B, S, 2, HEADS, HEAD_DIM))
            c_s.append(qkv_s[:, INNER:].reshape(DB, T, 2, HEADS, HEAD_DIM))
        g_final = norm_final if i == depth - 1 else None
        tail = (norm_ffn[i], bf(ffn_w1[i]), bf(ffn_w2[i]), norm_ple[i], bf(ple_w_proj[i]), bf(ple_w_gate[i]), g_final)
        xp = ffn_ple(xp, p_prompt[i].reshape(MP, -1), *tail, tm, 512)
        xs = ffn_ple(xs, p_sample[i].reshape(MS, -1), *tail, MS, 512)

    return (xp.reshape(B, S, D), xs.reshape(DB, T, D),
            jnp.stack(a_p[0]), jnp.stack(a_p[1]), jnp.stack(a_p[2]),
            jnp.stack(a_s[0]), jnp.stack(a_s[1]), jnp.stack(a_s[2]),
            jnp.stack(b_s_rows), jnp.stack(c_p), jnp.stack(c_s))
```

```python
import functools
import math

import jax
import jax.numpy as jnp
from jax import lax
from jax.experimental import pallas as pl
from jax.experimental.pallas import tpu as pltpu

F32 = jnp.float32
BF16 = jnp.bfloat16

HEAD_DIM = 64
HEADS = 16
INNER = HEADS * HEAD_DIM
ATTN_SCALE = 1.0 / math.sqrt(HEAD_DIM)
LOG2E = 1.4426950408889634
BLK = 128
A_WINDOWS = (128, 512, 2048)
A_DILATIONS = (1, 4, 16)
A_GROUPS = 3
A_TILE = 2048
A_UNITS = A_TILE // BLK
B_GROUPS = 8
CHUNK = 128
SB_QBLK = 512
SB_KBLK = 256
SB_PAGES_PER_STEP = 8
SAMPLE_HEAD_GROUPS = 4
N_MIXERS = 3
EPS = 1e-6
LANES = 128
SLABS = INNER // LANES
VMEM_LIMIT = 56 * 1024 * 1024


def _params(*sem):
    return pltpu.CompilerParams(dimension_semantics=sem, vmem_limit_bytes=VMEM_LIMIT)


def _rms(xf, g):
    return xf * lax.rsqrt(jnp.mean(xf * xf, axis=-1, keepdims=True) + EPS) * g


def _dot(a, b):
    return jnp.dot(a, b, preferred_element_type=F32)


def _dot_nt(a, b):
    return lax.dot_general(a, b, (((1,), (1,)), ((), ())), preferred_element_type=F32)


def _dot_split(a, b_bf16):
    rows = a.shape[0]
    hi = a.astype(BF16)
    lo = (a - hi.astype(F32)).astype(BF16)
    both = _dot(jnp.concatenate([hi, lo], axis=0), b_bf16)
    return both[:rows] + both[rows:]


def _softplus(z):
    return jnp.maximum(z, 0.0) + jnp.log(1.0 + jnp.exp2(jnp.abs(z) * (-LOG2E)))


def _later(n):
    return (lax.broadcasted_iota(jnp.int32, (n, n), 0) > lax.broadcasted_iota(jnp.int32, (n, n), 1)).astype(BF16)


def _eye(n):
    return lax.broadcasted_iota(jnp.int32, (n, n), 0) == lax.broadcasted_iota(jnp.int32, (n, n), 1)


def _row_to_col(row):
    n = row.shape[1]
    return jnp.sum(jnp.where(_eye(n), row, 0.0), axis=1, keepdims=True)


def _col_to_row(col):
    n = col.shape[0]
    return jnp.sum(jnp.where(_eye(n), col, 0.0), axis=0, keepdims=True)


def _head_sum(x, heads):
    return jnp.sum(x.reshape(heads, HEAD_DIM, x.shape[1]), axis=1)


def _head_rows(x, heads):
    return jnp.broadcast_to(x[:, None, :], (heads, HEAD_DIM, x.shape[1])).reshape(heads * HEAD_DIM, x.shape[1])


def _norm_matmul_kernel(x_ref, g_ref, w_ref, *rest, has_bias):
    if has_bias:
        b_ref, o_ref, h_ref = rest
    else:
        o_ref, h_ref = rest

    @pl.when(pl.program_id(1) == 0)
    def _():
        h_ref[...] = _rms(x_ref[...], g_ref[...]).astype(BF16)

    acc = _dot(h_ref[...], w_ref[...])
    if has_bias:
        acc = acc + b_ref[...]
    o_ref[...] = acc.astype(o_ref.dtype)


def norm_matmul(x, g, w, bias, out_dtype, tm, tn):
    M, D = x.shape
    N = w.shape[1]
    in_specs = [pl.BlockSpec((tm, D), lambda i, j: (i, 0)),
                pl.BlockSpec((1, D), lambda i, j: (0, 0)),
                pl.BlockSpec((D, tn), lambda i, j: (0, j))]
    args = [x, g.reshape(1, D), w]
    if bias is not None:
        in_specs.append(pl.BlockSpec((1, tn), lambda i, j: (0, j)))
        args.append(bias.reshape(1, N))
    return pl.pallas_call(
        functools.partial(_norm_matmul_kernel, has_bias=bias is not None),
        grid=(M // tm, N // tn),
        in_specs=in_specs,
        out_specs=pl.BlockSpec((tm, tn), lambda i, j: (i, j)),
        out_shape=jax.ShapeDtypeStruct((M, N), out_dtype),
        scratch_shapes=[pltpu.VMEM((tm, D), BF16)],
        compiler_params=_params("arbitrary", "arbitrary"),
        name="norm_matmul",
    )(*args)


def _norm_matmul_t_kernel(x_ref, g_ref, wt_ref, *rest, has_bias):
    if has_bias:
        b_ref, o_ref, h_ref = rest
    else:
        o_ref, h_ref = rest

    @pl.when(pl.program_id(2) == 0)
    def _():
        h_ref[...] = _rms(x_ref[...], g_ref[...]).astype(BF16)

    acc = _dot_nt(wt_ref[...], h_ref[...])
    if has_bias:
        acc = acc + b_ref[...]
    o_ref[...] = acc


def norm_matmul_t(x3, g, wt, bias, keep, tn):
    B, S, D = x3.shape
    N = wt.shape[0]
    tm = min(512, keep)
    first = (S - keep) // tm
    in_specs = [pl.BlockSpec((None, tm, D), lambda b, i, j: (b, first + i, 0)),
                pl.BlockSpec((1, D), lambda b, i, j: (0, 0)),
                pl.BlockSpec((tn, D), lambda b, i, j: (j, 0))]
    args = [x3, g.reshape(1, D), wt]
    if bias is not None:
        in_specs.append(pl.BlockSpec((tn, 1), lambda b, i, j: (j, 0)))
        args.append(bias.reshape(N, 1))
    return pl.pallas_call(
        functools.partial(_norm_matmul_t_kernel, has_bias=bias is not None),
        grid=(B, keep // tm, N // tn),
        in_specs=in_specs,
        out_specs=pl.BlockSpec((None, tn, tm), lambda b, i, j: (b, j, i)),
        out_shape=jax.ShapeDtypeStruct((B, N, keep), F32),
        scratch_shapes=[pltpu.VMEM((tm, D), BF16)],
        compiler_params=_params("arbitrary", "arbitrary", "arbitrary"),
        name="norm_matmul_t",
    )(*args)


def kv_rows_from_t(kv_t):
    B, _, keep = kv_t.shape
    return jnp.transpose(kv_t.reshape(B, 2, HEADS, HEAD_DIM, keep), (0, 4, 1, 2, 3))


def _outproj_kernel(x_ref, o_ref, w_ref, y_ref):
    y_ref[...] = x_ref[...] + _dot(o_ref[...].astype(BF16), w_ref[...])


def outproj(x, o, w, tm):
    M, D = x.shape
    K = o.shape[1]
    return pl.pallas_call(
        _outproj_kernel,
        grid=(M // tm,),
        in_specs=[pl.BlockSpec((tm, D), lambda i: (i, 0)),
                  pl.BlockSpec((tm, K), lambda i: (i, 0)),
                  pl.BlockSpec((K, D), lambda i: (0, 0))],
        out_specs=pl.BlockSpec((tm, D), lambda i: (i, 0)),
        out_shape=jax.ShapeDtypeStruct((M, D), F32),
        compiler_params=_params("arbitrary"),
        name="outproj",
    )(x, o, w)


def _merge_outproj_kernel(x_ref, o0_ref, o1_ref, o2_ref, l0_ref, l1_ref, l2_ref, w_ref, y_ref):
    l0, l1, l2 = l0_ref[...], l1_ref[...], l2_ref[...]
    m = jnp.maximum(jnp.maximum(l0, l1), l2)
    e0, e1, e2 = jnp.exp(l0 - m), jnp.exp(l1 - m), jnp.exp(l2 - m)
    den = e0 + e1 + e2
    expand = (lax.broadcasted_iota(jnp.int32, (LANES, INNER), 1) // HEAD_DIM
              == lax.broadcasted_iota(jnp.int32, (LANES, INNER), 0)).astype(BF16)
    o = (_dot_split(e0 / den, expand) * o0_ref[...].astype(F32)
         + _dot_split(e1 / den, expand) * o1_ref[...].astype(F32)
         + _dot_split(e2 / den, expand) * o2_ref[...].astype(F32))
    y_ref[...] = x_ref[...] + _dot(o.astype(BF16), w_ref[...])


def merge_outproj(x, outs, lses, w, tm):
    M, D = x.shape
    row = lambda width: pl.BlockSpec((tm, width), lambda i: (i, 0))
    return pl.pallas_call(
        _merge_outproj_kernel,
        grid=(M // tm,),
        in_specs=[row(D)] + [row(INNER)] * A_GROUPS + [row(LANES)] * A_GROUPS
        + [pl.BlockSpec((INNER, D), lambda i: (0, 0))],
        out_specs=row(D),
        out_shape=jax.ShapeDtypeStruct((M, D), F32),
        compiler_params=_params("arbitrary"),
        name="merge_outproj",
    )(x, *outs, *lses, w)


def _norm_streams_kernel(x_ref, g_ref, h_ref, slab_ref):
    grp = pl.program_id(1)

    @pl.when(grp == 0)
    def _():
        rows = 256
        for c in range(A_TILE // rows):
            sl = slice(c * rows, (c + 1) * rows)
            h = _rms(x_ref[sl, :], g_ref[...])
            h_ref[sl, :] = h.astype(BF16)
            for k in range(SLABS):
                slab_ref[k, sl, :] = h[:, k * LANES:(k + 1) * LANES]

    for g, dil in enumerate(A_DILATIONS):
        if dil == 1:
            continue
        per_stream = A_TILE // dil

        @pl.when(grp == g)
        def _(dil=dil, per_stream=per_stream):
            for r in range(dil):
                for k in range(SLABS):
                    h_ref[r * per_stream:(r + 1) * per_stream, k * LANES:(k + 1) * LANES] = (
                        slab_ref[k, pl.ds(r, per_stream, stride=dil), :].astype(BF16))


def norm_streams(x, g):
    M, D = x.shape
    assert A_DILATIONS[0] == 1
    return pl.pallas_call(
        _norm_streams_kernel,
        grid=(M // A_TILE, A_GROUPS),
        in_specs=[pl.BlockSpec((A_TILE, D), lambda i, g: (i, 0)), pl.BlockSpec((1, D), lambda i, g: (0, 0))],
        out_specs=pl.BlockSpec((None, A_TILE, D), lambda i, g: (g, i, 0)),
        out_shape=jax.ShapeDtypeStruct((A_GROUPS, M, D), BF16),
        scratch_shapes=[pltpu.VMEM((SLABS, A_TILE, LANES), F32)],
        compiler_params=_params("arbitrary", "arbitrary"),
        name="norm_streams",
    )(x, g.reshape(1, D))


def _group_matmul_kernel(h_ref, w_ref, o_ref):
    o_ref[...] = _dot(h_ref[...], w_ref[...]).astype(o_ref.dtype)


def group_matmul(h, w, tm):
    G, M, D = h.shape
    P = w.shape[2] // INNER
    return pl.pallas_call(
        _group_matmul_kernel,
        grid=(G, M // tm, P),
        in_specs=[pl.BlockSpec((None, tm, D), lambda g, i, j: (g, i, 0)),
                  pl.BlockSpec((None, D, INNER), lambda g, i, j: (g, 0, j))],
        out_specs=pl.BlockSpec((None, None, tm, INNER), lambda g, i, j: (g, j, i, 0)),
        out_shape=jax.ShapeDtypeStruct((G, P, M, INNER), BF16),
        compiler_params=_params("arbitrary", "arbitrary", "arbitrary"),
        name="group_matmul",
    )(h, w)


def _dil_attn_kernel(q_ref, kp_ref, kc_ref, vp_ref, vc_ref, o_ref, lse_ref, o_slab, lse_slab, *,
                     dil, tiles_per_seq):
    per_stream = A_UNITS // dil
    u = pl.program_id(1)
    r, t = u // per_stream, u % per_stream
    has_prev = (t > 0) | (pl.program_id(0) % tiles_per_seq > 0)
    row = lax.broadcasted_iota(jnp.int32, (BLK, 2 * BLK), 0)
    col = lax.broadcasted_iota(jnp.int32, (BLK, 2 * BLK), 1) & (BLK - 1)
    mask_c = col <= row
    mask_p = (col >= row) & has_prev
    lane = lax.broadcasted_iota(jnp.int32, (BLK, LANES), 1)
    first_head = lane < HEAD_DIM
    rows2 = lax.broadcasted_iota(jnp.int32, (2 * BLK, LANES), 0)
    lanes2 = lax.broadcasted_iota(jnp.int32, (2 * BLK, LANES), 1)
    ones_sel = ((rows2 < BLK) == (lanes2 < HEAD_DIM)).astype(BF16)
    lse_tile = jnp.zeros((BLK, LANES), F32)
    start = r + dil * BLK * t

    def two_heads(x):
        zero = jnp.zeros_like(x)
        return jnp.concatenate([jnp.where(first_head, x, zero), jnp.where(first_head, zero, x)], axis=0)

    for pair in range(HEADS // 2):
        sl = slice(pair * LANES, (pair + 1) * LANES)
        q2 = q_ref[:, sl] * ATTN_SCALE
        s_c = jnp.where(mask_c, _dot_nt(q2, two_heads(kc_ref[:, sl])), -jnp.inf)
        s_p = jnp.where(mask_p, _dot_nt(q2, two_heads(kp_ref[:, sl])), -jnp.inf)
        s_max = jnp.maximum(s_c, s_p)
        m0 = jnp.max(s_max[:, :BLK], axis=1, keepdims=True)
        m1 = jnp.max(s_max[:, BLK:], axis=1, keepdims=True)
        m = jnp.concatenate([jnp.broadcast_to(m0, (BLK, BLK)), jnp.broadcast_to(m1, (BLK, BLK))], axis=1)
        e_c = jnp.exp(s_c - m).astype(BF16)
        e_p = jnp.exp(s_p - m).astype(BF16)
        v_c = jnp.concatenate([two_heads(vc_ref[:, sl]), ones_sel], axis=1)
        v_p = jnp.concatenate([two_heads(vp_ref[:, sl]), ones_sel], axis=1)
        res = _dot(e_c, v_c) + _dot(e_p, v_p)
        den = res[:, LANES:]
        out = res[:, :LANES] / den
        if dil == 1:
            o_slab[pair, pl.ds(start, BLK), :] = out
        else:
            o_slab[pair, pl.ds(start, BLK, stride=dil), :] = out
        lse_tile = jnp.where(lane == 2 * pair, m0 + jnp.log(den[:, :1]), lse_tile)
        lse_tile = jnp.where(lane == 2 * pair + 1, m1 + jnp.log(den[:, HEAD_DIM:HEAD_DIM + 1]), lse_tile)
    if dil == 1:
        lse_slab[pl.ds(start, BLK), :] = lse_tile
    else:
        lse_slab[pl.ds(start, BLK, stride=dil), :] = lse_tile

    @pl.when(u == A_UNITS - 1)
    def _():
        for k in range(SLABS):
            o_ref[:, k * LANES:(k + 1) * LANES] = o_slab[k].astype(o_ref.dtype)
        lse_ref[...] = lse_slab[...]


def dil_attn(qkv, g, dil, tiles_per_seq):
    M = qkv.shape[2]
    per_stream = A_UNITS // dil

    def cur(part):
        return pl.BlockSpec((None, None, BLK, INNER), lambda c, u: (g, part, c * A_UNITS + u, 0))

    def prev(part):
        def index(c, u):
            blk = c * A_UNITS + u
            t = u % per_stream
            in_tile = blk - 1
            in_prev_tile = blk - A_UNITS + per_stream - 1
            has_prev_tile = c % tiles_per_seq > 0
            return (g, part, jnp.where(t > 0, in_tile, jnp.where(has_prev_tile, in_prev_tile, blk)), 0)
        return pl.BlockSpec((None, None, BLK, INNER), index)

    return pl.pallas_call(
        functools.partial(_dil_attn_kernel, dil=dil, tiles_per_seq=tiles_per_seq),
        grid=(M // A_TILE, A_UNITS),
        in_specs=[cur(0), prev(1), cur(1), prev(2), cur(2)],
        out_specs=[pl.BlockSpec((A_TILE, INNER), lambda c, u: (c, 0)),
                   pl.BlockSpec((A_TILE, LANES), lambda c, u: (c, 0))],
        out_shape=[jax.ShapeDtypeStruct((M, INNER), BF16), jax.ShapeDtypeStruct((M, LANES), F32)],
        scratch_shapes=[pltpu.VMEM((SLABS, A_TILE, LANES), F32), pltpu.VMEM((A_TILE, LANES), F32)],
        compiler_params=_params("arbitrary", "arbitrary"),
        name=f"dil_attn_d{dil}",
    )(qkv, qkv, qkv, qkv, qkv)


def _dil_sample_kernel(*refs):
    new_refs = refs[:3 * A_GROUPS]
    cache_refs = refs[3 * A_GROUPS:5 * A_GROUPS]
    o_ref = refs[5 * A_GROUPS]
    heads = HEADS // SAMPLE_HEAD_GROUPS
    scores, new_scores = [], []
    for g, dil in enumerate(A_DILATIONS):
        q_col = _row_to_col(new_refs[g][...])
        k_col = _row_to_col(new_refs[A_GROUPS + g][...])
        kt = cache_refs[2 * g][...]
        s = _head_sum(kt * q_col, heads) * ATTN_SCALE
        pos = lax.broadcasted_iota(jnp.int32, s.shape, 1)
        scores.append(jnp.where((pos & (dil - 1)) == 0, s, -jnp.inf))
        new_scores.append(_head_sum(q_col * k_col, heads) * ATTN_SCALE)
    m = new_scores[0]
    for g in range(A_GROUPS):
        m = jnp.maximum(m, jnp.maximum(new_scores[g], jnp.max(scores[g], axis=1, keepdims=True)))
    num = jnp.zeros((heads * HEAD_DIM, 1), F32)
    den = jnp.zeros((heads, 1), F32)
    for g in range(A_GROUPS):
        v_col = _row_to_col(new_refs[2 * A_GROUPS + g][...])
        e = jnp.exp(scores[g] - m)
        e_new = jnp.exp(new_scores[g] - m)
        vt = cache_refs[2 * g + 1][...]
        num = num + jnp.sum(vt * _head_rows(e, heads), axis=1, keepdims=True) + _head_rows(e_new, heads) * v_col
        den = den + jnp.sum(e, axis=1, keepdims=True) + e_new
    o_ref[...] = _col_to_row(num / _head_rows(den, heads))


def dil_sample(qkv, caches, layer):
    DB, W = qkv.shape
    hgs = SAMPLE_HEAD_GROUPS
    width = INNER // hgs
    new_specs = [pl.BlockSpec((None, 1, width), lambda b, h, c=(part * A_GROUPS + g) * hgs: (b, 0, c + h))
                 for part in range(3) for g in range(A_GROUPS)]
    views, specs = [], []
    for cache, win, dil in zip(caches, A_WINDOWS, A_DILATIONS):
        assert cache.shape[2] == win and win // dil == BLK and dil & (dil - 1) == 0
        view = jnp.transpose(cache, (0, 1, 3, 4, 5, 2)).reshape(cache.shape[0], DB, 2 * INNER, win)
        views += [view, view]
        specs += [pl.BlockSpec((None, None, width, win), lambda b, h: (layer, b, h, 0)),
                  pl.BlockSpec((None, None, width, win), lambda b, h: (layer, b, hgs + h, 0))]
    qkv3 = qkv.reshape(DB, 1, W)
    o = pl.pallas_call(
        _dil_sample_kernel,
        grid=(DB, hgs),
        in_specs=new_specs + specs,
        out_specs=pl.BlockSpec((None, 1, width), lambda b, h: (b, 0, h)),
        out_shape=jax.ShapeDtypeStruct((DB, 1, INNER), F32),
        compiler_params=_params("arbitrary", "arbitrary"),
        name="dil_sample",
    )(*([qkv3] * (3 * A_GROUPS)), *views)
    return o.reshape(DB, INNER)


def _sgu_front(x, g_ref, win_ref, lng_ref, lnb_ref):
    z = jax.nn.gelu(_dot(_rms(x, g_ref[...]).astype(BF16), win_ref[...]))
    width = z.shape[1] // 2
    u, vr = z[:, :width], z[:, width:]
    mu = jnp.mean(vr, axis=-1, keepdims=True)
    var = jnp.mean(jnp.square(vr - mu), axis=-1, keepdims=True)
    return u, (vr - mu) * lax.rsqrt(var + EPS) * lng_ref[...] + lnb_ref[...]


def _sgu_prompt_kernel(x_ref, g_ref, win_ref, lng_ref, lnb_ref, ws_ref, bs_ref, wout_ref, y_ref, us_ref):
    x = x_ref[...]
    u, v = _sgu_front(x, g_ref, win_ref, lng_ref, lnb_ref)
    causal = (lax.broadcasted_iota(jnp.int32, (CHUNK, CHUNK), 0)
              >= lax.broadcasted_iota(jnp.int32, (CHUNK, CHUNK), 1))
    for grp in range(B_GROUPS):
        cols = slice(grp * CHUNK, (grp + 1) * CHUNK)
        wm = jnp.where(causal, ws_ref[grp], 0.0).astype(BF16)
        for c in range(x.shape[0] // CHUNK):
            rows = slice(c * CHUNK, (c + 1) * CHUNK)
            s = _dot(wm, v[rows, cols].astype(BF16)) + bs_ref[:, cols]
            us_ref[rows, cols] = (u[rows, cols] * s).astype(BF16)
    y_ref[...] = x + _dot(us_ref[...], wout_ref[...])


def sgu_prompt(x, g, w_in, ln_g, ln_b, w_s, b_s, w_out, tm):
    M, D = x.shape
    width = w_out.shape[0]
    bias = jnp.repeat(b_s.T, width // B_GROUPS, axis=1)
    const = lambda shape: pl.BlockSpec(shape, lambda i: (0,) * len(shape))
    return pl.pallas_call(
        _sgu_prompt_kernel,
        grid=(M // tm,),
        in_specs=[pl.BlockSpec((tm, D), lambda i: (i, 0)), const((1, D)), const((D, 2 * width)),
                  const((1, width)), const((1, width)), const((B_GROUPS, CHUNK, CHUNK)),
                  const((CHUNK, width)), const((width, D))],
        out_specs=pl.BlockSpec((tm, D), lambda i: (i, 0)),
        out_shape=jax.ShapeDtypeStruct((M, D), F32),
        scratch_shapes=[pltpu.VMEM((tm, width), BF16)],
        compiler_params=_params("arbitrary"),
        name="sgu_prompt",
    )(x, g.reshape(1, D), w_in, ln_g.reshape(1, width), ln_b.reshape(1, width), w_s, bias, w_out)


def _sgu_sample_kernel(x_ref, g_ref, win_ref, lng_ref, lnb_ref, sw_ref, sb_ref, wout_ref, y_ref, v_ref):
    x = x_ref[...]
    u, v = _sgu_front(x, g_ref, win_ref, lng_ref, lnb_ref)
    s = sw_ref[...] * v + sb_ref[...]
    y_ref[...] = x + _dot((u * s).astype(BF16), wout_ref[...])
    v_ref[...] = v


def sgu_sample(x, g, w_in, ln_g, ln_b, w_s, b_s, w_out):
    M, D = x.shape
    width = w_out.shape[0]
    sw = jnp.repeat(w_s[:, 0, 0], width // B_GROUPS).reshape(1, width)
    sb = jnp.repeat(b_s[:, 0], width // B_GROUPS).reshape(1, width)
    return pl.pallas_call(
        _sgu_sample_kernel,
        out_shape=[jax.ShapeDtypeStruct((M, D), F32), jax.ShapeDtypeStruct((M, width), F32)],
        compiler_params=pltpu.CompilerParams(vmem_limit_bytes=VMEM_LIMIT),
        name="sgu_sample",
    )(x, g.reshape(1, D), w_in, ln_g.reshape(1, width), ln_b.reshape(1, width), sw, sb, w_out)


def _sb_attn_kernel(q_ref, k_ref, v_ref, o_ref):
    i = pl.program_id(2)
    nq, nk = SB_QBLK, SB_KBLK
    lane = lax.broadcasted_iota(jnp.int32, (nq, LANES), 1)
    first_head = lane < HEAD_DIM
    later = _later(nk)
    q2 = q_ref[...] * ATTN_SCALE
    zero = jnp.zeros_like(q2)
    qs = jnp.concatenate([jnp.where(first_head, q2, zero), jnp.where(first_head, zero, q2)], axis=0)
    row = lax.broadcasted_iota(jnp.int32, (2 * nq, nk), 0) & (nq - 1)
    col = lax.broadcasted_iota(jnp.int32, (2 * nq, nk), 1)

    def block(kb, o, c, offset):
        start = pl.multiple_of(kb * nk, nk)
        z = _dot_nt(qs, k_ref[pl.ds(start, nk), :])
        sp = _softplus(z)
        if offset is not None:
            valid = col + offset < row
            sp = jnp.where(valid, sp, 0.0)
        a = jnp.exp(z - sp - _dot_split(sp, later) - c)
        if offset is not None:
            a = jnp.where(valid, a, 0.0)
        o = o + _dot(a.astype(BF16), v_ref[pl.ds(start, nk), :])
        return o, c + jnp.sum(sp, axis=1, keepdims=True)

    o = jnp.zeros((2 * nq, LANES), F32)
    c = jnp.zeros((2 * nq, 1), F32)
    first_kb = i * (nq // nk)
    for d in reversed(range(nq // nk)):
        o, c = block(first_kb + d, o, c, d * nk)
    o, c = lax.fori_loop(0, first_kb, lambda s, oc: block(first_kb - 1 - s, oc[0], oc[1], None), (o, c))
    o_ref[...] = jnp.where(first_head, o[:nq], o[nq:]).astype(o_ref.dtype)


def sb_attn(qkv):
    B, S, _ = qkv.shape
    pairs = INNER // LANES
    return pl.pallas_call(
        _sb_attn_kernel,
        grid=(B, pairs, S // SB_QBLK),
        in_specs=[pl.BlockSpec((None, SB_QBLK, LANES), lambda b, p, i: (b, i, p)),
                  pl.BlockSpec((None, S, LANES), lambda b, p, i: (b, 0, pairs + p)),
                  pl.BlockSpec((None, S, LANES), lambda b, p, i: (b, 0, 2 * pairs + p))],
        out_specs=pl.BlockSpec((None, SB_QBLK, LANES), lambda b, p, i: (b, i, p)),
        out_shape=jax.ShapeDtypeStruct((B, S, INNER), BF16),
        compiler_params=_params("arbitrary", "arbitrary", "arbitrary"),
        name="sb_attn",
    )(qkv, qkv, qkv)


def _sb_sample_kernel(pt_ref, q_ref, *refs):
    del pt_ref
    page_refs = refs[:SB_PAGES_PER_STEP]
    o_ref, qcol_ref, acc_ref, c_ref = refs[SB_PAGES_PER_STEP:]
    p = pl.program_id(1)
    n = page_refs[0].shape[1]

    @pl.when(p == 0)
    def _():
        for k in range(SLABS):
            rows = slice(k * LANES, (k + 1) * LANES)
            qcol_ref[rows, :] = jnp.broadcast_to(_row_to_col(q_ref[:, rows]), (LANES, n))
        acc_ref[...] = jnp.zeros_like(acc_ref)
        c_ref[...] = jnp.zeros_like(c_ref)

    later = _later(n)
    for page_ref in page_refs:
        z = _head_sum(page_ref[:INNER, :] * qcol_ref[...], HEADS) * ATTN_SCALE
        sp = _softplus(z)
        a = jnp.exp(z - sp - _dot_split(sp, later) - c_ref[...])
        acc_ref[...] += page_ref[INNER:, :] * _head_rows(a, HEADS)
        c_ref[...] += jnp.sum(sp, axis=1, keepdims=True)

    @pl.when(p == pl.num_programs(1) - 1)
    def _():
        for k in range(SLABS):
            rows = slice(k * LANES, (k + 1) * LANES)
            o_ref[:, rows] = _col_to_row(jnp.sum(acc_ref[rows, :], axis=1, keepdims=True))


def sb_sample(q, pool, layer, page_table):
    DB = q.shape[0]
    layers, n_pool, page = pool.shape[:3]
    n_pages = page_table.shape[1]
    pps = SB_PAGES_PER_STEP
    assert n_pages % pps == 0
    pool_t = jnp.transpose(pool, (0, 1, 3, 4, 5, 2)).reshape(layers, n_pool, 2 * INNER, page)

    def page_spec(k):
        return pl.BlockSpec((None, None, 2 * INNER, page),
                            lambda b, p, pt: (layer, pt[b, n_pages - 1 - (p * pps + k)], 0, 0))

    o = pl.pallas_call(
        _sb_sample_kernel,
        grid_spec=pltpu.PrefetchScalarGridSpec(
            num_scalar_prefetch=1,
            grid=(DB, n_pages // pps),
            in_specs=[pl.BlockSpec((None, 1, INNER), lambda b, p, pt: (b, 0, 0))]
            + [page_spec(k) for k in range(pps)],
            out_specs=pl.BlockSpec((None, 1, INNER), lambda b, p, pt: (b, 0, 0)),
            scratch_shapes=[pltpu.VMEM((INNER, page), F32), pltpu.VMEM((INNER, page), F32),
                            pltpu.VMEM((HEADS, page), F32)]),
        out_shape=jax.ShapeDtypeStruct((DB, 1, INNER), F32),
        compiler_params=_params("arbitrary", "arbitrary"),
        name="sb_sample",
    )(page_table, q.reshape(DB, 1, INNER), *([pool_t] * pps))
    return o.reshape(DB, INNER)


def _ffn_ple_kernel(x_ref, p_ref, gf_ref, w1_ref, w2_ref, gp_ref, wproj_ref, wgate_ref, *rest, final):
    if final:
        gfin_ref, y_ref, h_ref, acc_ref = rest
    else:
        y_ref, h_ref, acc_ref = rest
    f = pl.program_id(1)

    @pl.when(f == 0)
    def _():
        h_ref[...] = _rms(x_ref[...], gf_ref[...]).astype(BF16)
        acc_ref[...] = jnp.zeros_like(acc_ref)

    a = jnp.square(jnp.maximum(_dot(h_ref[...], w1_ref[...]), 0.0))
    acc_ref[...] += _dot(a.astype(BF16), w2_ref[...])

    @pl.when(f == pl.num_programs(1) - 1)
    def _():
        x2 = x_ref[...] + acc_ref[...]
        gate = jax.nn.sigmoid(_dot(_rms(x2, gp_ref[...]).astype(BF16), wgate_ref[...]))
        y = x2 + _dot(p_ref[...].astype(BF16), wproj_ref[...]) * gate
        if final:
            y = _rms(y, gfin_ref[...])
        y_ref[...] = y


def ffn_ple(x, layer, p, g_ffn, w1, w2, g_ple, w_proj, w_gate, g_final, tm, tf):
    M, D = x.shape
    F = w1.shape[2]
    P = p.shape[2]
    const = lambda shape: pl.BlockSpec(shape, lambda i, f: (0, 0))
    in_specs = [pl.BlockSpec((tm, D), lambda i, f: (i, 0)),
                pl.BlockSpec((None, tm, P), lambda i, f: (layer, i, 0)),
                const((1, D)),
                pl.BlockSpec((None, D, tf), lambda i, f: (layer, 0, f)),
                pl.BlockSpec((None, tf, D), lambda i, f: (layer, f, 0)),
                const((1, D)),
                pl.BlockSpec((None, P, D), lambda i, f: (layer, 0, 0)),
                pl.BlockSpec((None, D, D), lambda i, f: (layer, 0, 0))]
    args = [x, p, g_ffn.reshape(1, D), w1, w2, g_ple.reshape(1, D), w_proj, w_gate]
    if g_final is not None:
        in_specs.append(const((1, D)))
        args.append(g_final.reshape(1, D))
    return pl.pallas_call(
        functools.partial(_ffn_ple_kernel, final=g_final is not None),
        grid=(M // tm, F // tf),
        in_specs=in_specs,
        out_specs=pl.BlockSpec((tm, D), lambda i, f: (i, 0)),
        out_shape=jax.ShapeDtypeStruct((M, D), F32),
        scratch_shapes=[pltpu.VMEM((tm, D), BF16), pltpu.VMEM((tm, D), F32)],
        compiler_params=_params("arbitrary", "arbitrary"),
        name="ffn_ple",
    )(*args)


def kernel(x_prompt, x_sample, p_prompt, p_sample, cache_a_w128, cache_a_w512, cache_a_w2048, cache_c_kv, page_table, norm_mix, norm_ffn, norm_ple, norm_final, a_w_in, a_w_out, b_w_in, b_ln_g, b_ln_b, b_w_s, b_b_s, b_w_out, c_w_in, c_b_in, c_w_out, ffn_w1, ffn_w2, ple_w_proj, ple_w_gate):
    B, S, D = x_prompt.shape
    DB, T, _ = x_sample.shape
    assert T == 1, "the sample path handles one new token per sequence"
    assert S % A_TILE == 0 and S % SB_QBLK == 0
    depth = norm_mix.shape[0]
    MP, MS = B * S, DB * T
    tm = min(512, MP)
    bf = lambda w: w.astype(BF16)

    xp = x_prompt.reshape(MP, D)
    xs = x_sample.reshape(MS, D)
    caches_a = (cache_a_w128, cache_a_w512, cache_a_w2048)
    w1_all, w2_all, w_proj_all, w_gate_all = bf(ffn_w1), bf(ffn_w2), bf(ple_w_proj), bf(ple_w_gate)
    a_p = [[] for _ in A_WINDOWS]
    a_s = [[] for _ in A_WINDOWS]
    b_s_rows, c_p, c_s = [], [], []

    for i in range(depth):
        kind, j = i % N_MIXERS, i // N_MIXERS
        if kind == 0:
            w_in, w_out = bf(a_w_in[j]), bf(a_w_out[j])
            w4 = w_in.reshape(D, 3, A_GROUPS, INNER)
            w_groups = jnp.transpose(w4, (2, 0, 1, 3)).reshape(A_GROUPS, D, 3 * INNER)
            qkv_p = group_matmul(norm_streams(xp, norm_mix[i]), w_groups, 1024)
            outs, lses = zip(*[dil_attn(qkv_p, g, dil, S // A_TILE) for g, dil in enumerate(A_DILATIONS)])
            x3 = xp.reshape(B, S, D)
            for g, win in enumerate(A_WINDOWS):
                wt_kv = jnp.transpose(w4[:, 1:, g, :].reshape(D, 2 * INNER))
                a_p[g].append(kv_rows_from_t(norm_matmul_t(x3, norm_mix[i], wt_kv, None, min(win, S), 1024)))
            xp = merge_outproj(xp, outs, lses, w_out, tm)
            qkv_s = norm_matmul(xs, norm_mix[i], w_in, None, F32, MS, 1024)
            xs = outproj(xs, dil_sample(qkv_s, caches_a, j), w_out, MS)
            kv_s = qkv_s.reshape(DB, T, 3, A_GROUPS, HEADS, HEAD_DIM)
            for g in range(A_GROUPS):
                a_s[g].append(jnp.stack([kv_s[:, :, 1, g], kv_s[:, :, 2, g]], axis=2))
        elif kind == 1:
            w_in, w_out = bf(b_w_in[j]), bf(b_w_out[j])
            xp = sgu_prompt(xp, norm_mix[i], w_in, b_ln_g[j], b_ln_b[j], b_w_s[j], b_b_s[j], w_out, min(256, MP))
            xs, v_s = sgu_sample(xs, norm_mix[i], w_in, b_ln_g[j], b_ln_b[j], b_w_s[j], b_b_s[j], w_out)
            b_s_rows.append(v_s.reshape(DB, T, -1))
        else:
            w_in, w_out = bf(c_w_in[j]), bf(c_w_out[j])
            qkv_p = norm_matmul(xp, norm_mix[i], w_in, c_b_in[j], BF16, tm, 1024)
            kv_t = norm_matmul_t(xp.reshape(B, S, D), norm_mix[i], jnp.transpose(w_in[:, INNER:]),
                                 c_b_in[j][INNER:], S, 1024)
            c_p.append(kv_rows_from_t(kv_t))
            qkv_s = norm_matmul(xs, norm_mix[i], w_in, c_b_in[j], F32, MS, 1024)
            o_p = sb_attn(qkv_p.reshape(B, S, -1))
            xp = outproj(xp, o_p.reshape(MP, INNER), w_out, tm)
            xs = outproj(xs, sb_sample(qkv_s[:, :INNER], cache_c_kv, j, page_table), w_out, MS)
            c_s.append(qkv_s[:, INNER:].reshape(DB, T, 2, HEADS, HEAD_DIM))
        g_final = norm_final if i == depth - 1 else None
        tail = (norm_ffn[i], w1_all, w2_all, norm_ple[i], w_proj_all, w_gate_all, g_final)
        xp = ffn_ple(xp, i, p_prompt.reshape(depth, MP, -1), *tail, tm, 512)
        xs = ffn_ple(xs, i, p_sample.reshape(depth, MS, -1), *tail, MS, 512)

    return (xp.reshape(B, S, D), xs.reshape(DB, T, D),
            jnp.stack(a_p[0]), jnp.stack(a_p[1]), jnp.stack(a_p[2]),
            jnp.stack(a_s[0]), jnp.stack(a_s[1]), jnp.stack(a_s[2]),
            jnp.stack(b_s_rows), jnp.stack(c_p), jnp.stack(c_s))
```

```python
import functools
import math

import jax
import jax.numpy as jnp
from jax import lax
from jax.experimental import pallas as pl
from jax.experimental.pallas import tpu as pltpu

F32 = jnp.float32
BF16 = jnp.bfloat16

HEAD_DIM = 64
HEADS = 16
INNER = HEADS * HEAD_DIM
ATTN_SCALE = 1.0 / math.sqrt(HEAD_DIM)
LOG2E = 1.4426950408889634
BLK = 128
A_WINDOWS = (128, 512, 2048)
A_DILATIONS = (1, 4, 16)
A_GROUPS = 3
A_TILE = 2048
A_UNITS = A_TILE // BLK
B_GROUPS = 8
CHUNK = 128
SB_QBLK = 1024
SB_KBLK = 256
SB_PAGES_PER_STEP = 8
FFN_CHUNK = 512
SAMPLE_HEAD_GROUPS = 4
N_MIXERS = 3
EPS = 1e-6
LANES = 128
SLABS = INNER // LANES
VMEM_LIMIT = 56 * 1024 * 1024


def _params(*sem):
    return pltpu.CompilerParams(dimension_semantics=sem, vmem_limit_bytes=VMEM_LIMIT)


def _rms(xf, g):
    return xf * lax.rsqrt(jnp.mean(xf * xf, axis=-1, keepdims=True) + EPS) * g


def _dot(a, b):
    return jnp.dot(a, b, preferred_element_type=F32)


def _dot_nt(a, b):
    return lax.dot_general(a, b, (((1,), (1,)), ((), ())), preferred_element_type=F32)


def _dot_split(a, b_bf16):
    rows = a.shape[0]
    hi = a.astype(BF16)
    lo = (a - hi.astype(F32)).astype(BF16)
    both = _dot(jnp.concatenate([hi, lo], axis=0), b_bf16)
    return both[:rows] + both[rows:]


def _softplus(z):
    return jnp.maximum(z, 0.0) + jnp.log(1.0 + jnp.exp2(jnp.abs(z) * (-LOG2E)))


def _later(n):
    return (lax.broadcasted_iota(jnp.int32, (n, n), 0) > lax.broadcasted_iota(jnp.int32, (n, n), 1)).astype(BF16)


def _eye(n):
    return lax.broadcasted_iota(jnp.int32, (n, n), 0) == lax.broadcasted_iota(jnp.int32, (n, n), 1)


def _row_to_col(row):
    n = row.shape[1]
    return jnp.sum(jnp.where(_eye(n), row, 0.0), axis=1, keepdims=True)


def _col_to_row(col):
    n = col.shape[0]
    return jnp.sum(jnp.where(_eye(n), col, 0.0), axis=0, keepdims=True)


def _head_sum(x, heads):
    return jnp.sum(x.reshape(heads, HEAD_DIM, x.shape[1]), axis=1)


def _head_rows(x, heads):
    return jnp.broadcast_to(x[:, None, :], (heads, HEAD_DIM, x.shape[1])).reshape(heads * HEAD_DIM, x.shape[1])


def _norm_matmul_kernel(x_ref, g_ref, w_ref, *rest, has_bias):
    if has_bias:
        b_ref, o_ref, h_ref = rest
    else:
        o_ref, h_ref = rest

    @pl.when(pl.program_id(1) == 0)
    def _():
        h_ref[...] = _rms(x_ref[...], g_ref[...]).astype(BF16)

    acc = _dot(h_ref[...], w_ref[...])
    if has_bias:
        acc = acc + b_ref[...]
    o_ref[...] = acc.astype(o_ref.dtype)


def norm_matmul(x, g, w, bias, out_dtype, tm, tn):
    M, D = x.shape
    N = w.shape[1]
    in_specs = [pl.BlockSpec((tm, D), lambda i, j: (i, 0)),
                pl.BlockSpec((1, D), lambda i, j: (0, 0)),
                pl.BlockSpec((D, tn), lambda i, j: (0, j))]
    args = [x, g.reshape(1, D), w]
    if bias is not None:
        in_specs.append(pl.BlockSpec((1, tn), lambda i, j: (0, j)))
        args.append(bias.reshape(1, N))
    return pl.pallas_call(
        functools.partial(_norm_matmul_kernel, has_bias=bias is not None),
        grid=(M // tm, N // tn),
        in_specs=in_specs,
        out_specs=pl.BlockSpec((tm, tn), lambda i, j: (i, j)),
        out_shape=jax.ShapeDtypeStruct((M, N), out_dtype),
        scratch_shapes=[pltpu.VMEM((tm, D), BF16)],
        compiler_params=_params("arbitrary", "arbitrary"),
        name="norm_matmul",
    )(*args)


def _norm_matmul_t_kernel(x_ref, g_ref, wt_ref, *rest, has_bias):
    if has_bias:
        b_ref, o_ref, h_ref = rest
    else:
        o_ref, h_ref = rest

    @pl.when(pl.program_id(2) == 0)
    def _():
        h_ref[...] = _rms(x_ref[...], g_ref[...]).astype(BF16)

    acc = _dot_nt(wt_ref[...], h_ref[...])
    if has_bias:
        acc = acc + b_ref[...]
    o_ref[...] = acc


def norm_matmul_t(x3, g, wt, bias, keep, tn):
    B, S, D = x3.shape
    N = wt.shape[0]
    tm = min(512, keep)
    first = (S - keep) // tm
    in_specs = [pl.BlockSpec((None, tm, D), lambda b, i, j: (b, first + i, 0)),
                pl.BlockSpec((1, D), lambda b, i, j: (0, 0)),
                pl.BlockSpec((tn, D), lambda b, i, j: (j, 0))]
    args = [x3, g.reshape(1, D), wt]
    if bias is not None:
        in_specs.append(pl.BlockSpec((tn, 1), lambda b, i, j: (j, 0)))
        args.append(bias.reshape(N, 1))
    return pl.pallas_call(
        functools.partial(_norm_matmul_t_kernel, has_bias=bias is not None),
        grid=(B, keep // tm, N // tn),
        in_specs=in_specs,
        out_specs=pl.BlockSpec((None, tn, tm), lambda b, i, j: (b, j, i)),
        out_shape=jax.ShapeDtypeStruct((B, N, keep), F32),
        scratch_shapes=[pltpu.VMEM((tm, D), BF16)],
        compiler_params=_params("arbitrary", "arbitrary", "arbitrary"),
        name="norm_matmul_t",
    )(*args)


def kv_rows_from_t(kv_t):
    B, _, keep = kv_t.shape
    return jnp.transpose(kv_t.reshape(B, 2, HEADS, HEAD_DIM, keep), (0, 4, 1, 2, 3))


def _outproj_kernel(x_ref, o_ref, w_ref, y_ref):
    y_ref[...] = x_ref[...] + _dot(o_ref[...].astype(BF16), w_ref[...])


def outproj(x, o, w, tm):
    M, D = x.shape
    K = o.shape[1]
    return pl.pallas_call(
        _outproj_kernel,
        grid=(M // tm,),
        in_specs=[pl.BlockSpec((tm, D), lambda i: (i, 0)),
                  pl.BlockSpec((tm, K), lambda i: (i, 0)),
                  pl.BlockSpec((K, D), lambda i: (0, 0))],
        out_specs=pl.BlockSpec((tm, D), lambda i: (i, 0)),
        out_shape=jax.ShapeDtypeStruct((M, D), F32),
        compiler_params=_params("arbitrary"),
        name="outproj",
    )(x, o, w)


def _merge_outproj_kernel(x_ref, o0_ref, o1_ref, o2_ref, l0_ref, l1_ref, l2_ref, w_ref, y_ref):
    l0, l1, l2 = l0_ref[...], l1_ref[...], l2_ref[...]
    m = jnp.maximum(jnp.maximum(l0, l1), l2)
    e0, e1, e2 = jnp.exp(l0 - m), jnp.exp(l1 - m), jnp.exp(l2 - m)
    den = e0 + e1 + e2
    expand = (lax.broadcasted_iota(jnp.int32, (LANES, INNER), 1) // HEAD_DIM
              == lax.broadcasted_iota(jnp.int32, (LANES, INNER), 0)).astype(BF16)
    o = (_dot_split(e0 / den, expand) * o0_ref[...].astype(F32)
         + _dot_split(e1 / den, expand) * o1_ref[...].astype(F32)
         + _dot_split(e2 / den, expand) * o2_ref[...].astype(F32))
    y_ref[...] = x_ref[...] + _dot(o.astype(BF16), w_ref[...])


def merge_outproj(x, outs, lses, w, tm):
    M, D = x.shape
    row = lambda width: pl.BlockSpec((tm, width), lambda i: (i, 0))
    return pl.pallas_call(
        _merge_outproj_kernel,
        grid=(M // tm,),
        in_specs=[row(D)] + [row(INNER)] * A_GROUPS + [row(LANES)] * A_GROUPS
        + [pl.BlockSpec((INNER, D), lambda i: (0, 0))],
        out_specs=row(D),
        out_shape=jax.ShapeDtypeStruct((M, D), F32),
        compiler_params=_params("arbitrary"),
        name="merge_outproj",
    )(x, *outs, *lses, w)


def _norm_streams_kernel(x_ref, g_ref, h_ref, slab_ref):
    grp = pl.program_id(1)

    @pl.when(grp == 0)
    def _():
        rows = 256
        for c in range(A_TILE // rows):
            sl = slice(c * rows, (c + 1) * rows)
            h = _rms(x_ref[sl, :], g_ref[...])
            h_ref[sl, :] = h.astype(BF16)
            for k in range(SLABS):
                slab_ref[k, sl, :] = h[:, k * LANES:(k + 1) * LANES]

    for g, dil in enumerate(A_DILATIONS):
        if dil == 1:
            continue
        per_stream = A_TILE // dil

        @pl.when(grp == g)
        def _(dil=dil, per_stream=per_stream):
            for r in range(dil):
                for k in range(SLABS):
                    h_ref[r * per_stream:(r + 1) * per_stream, k * LANES:(k + 1) * LANES] = (
                        slab_ref[k, pl.ds(r, per_stream, stride=dil), :].astype(BF16))


def norm_streams(x, g):
    M, D = x.shape
    assert A_DILATIONS[0] == 1
    return pl.pallas_call(
        _norm_streams_kernel,
        grid=(M // A_TILE, A_GROUPS),
        in_specs=[pl.BlockSpec((A_TILE, D), lambda i, g: (i, 0)), pl.BlockSpec((1, D), lambda i, g: (0, 0))],
        out_specs=pl.BlockSpec((None, A_TILE, D), lambda i, g: (g, i, 0)),
        out_shape=jax.ShapeDtypeStruct((A_GROUPS, M, D), BF16),
        scratch_shapes=[pltpu.VMEM((SLABS, A_TILE, LANES), F32)],
        compiler_params=_params("arbitrary", "arbitrary"),
        name="norm_streams",
    )(x, g.reshape(1, D))


def _group_matmul_kernel(h_ref, w_ref, o_ref):
    o_ref[...] = _dot(h_ref[...], w_ref[...]).astype(o_ref.dtype)


def group_matmul(h, w, layer, tm):
    G, M, D = h.shape
    P = w.shape[2] // (G * INNER)
    return pl.pallas_call(
        _group_matmul_kernel,
        grid=(G, M // tm, P),
        in_specs=[pl.BlockSpec((None, tm, D), lambda g, i, j: (g, i, 0)),
                  pl.BlockSpec((None, D, INNER), lambda g, i, j: (layer, 0, j * G + g))],
        out_specs=pl.BlockSpec((None, None, tm, INNER), lambda g, i, j: (g, j, i, 0)),
        out_shape=jax.ShapeDtypeStruct((G, P, M, INNER), BF16),
        compiler_params=_params("arbitrary", "arbitrary", "arbitrary"),
        name="group_matmul",
    )(h, w)


def _dil_attn_kernel(q_ref, kp_ref, kc_ref, vp_ref, vc_ref, o_ref, lse_ref, o_slab, lse_slab, *,
                     dil, tiles_per_seq):
    per_stream = A_UNITS // dil
    u = pl.program_id(1)
    r, t = u // per_stream, u % per_stream
    has_prev = (t > 0) | (pl.program_id(0) % tiles_per_seq > 0)
    row = lax.broadcasted_iota(jnp.int32, (BLK, 2 * BLK), 0)
    col = lax.broadcasted_iota(jnp.int32, (BLK, 2 * BLK), 1) & (BLK - 1)
    mask_c = col <= row
    mask_p = (col >= row) & has_prev
    lane = lax.broadcasted_iota(jnp.int32, (BLK, LANES), 1)
    first_head = lane < HEAD_DIM
    rows2 = lax.broadcasted_iota(jnp.int32, (2 * BLK, LANES), 0)
    lanes2 = lax.broadcasted_iota(jnp.int32, (2 * BLK, LANES), 1)
    ones_sel = ((rows2 < BLK) == (lanes2 < HEAD_DIM)).astype(BF16)
    lse_tile = jnp.zeros((BLK, LANES), F32)
    start = r + dil * BLK * t

    def two_heads(x):
        zero = jnp.zeros_like(x)
        return jnp.concatenate([jnp.where(first_head, x, zero), jnp.where(first_head, zero, x)], axis=0)

    for pair in range(HEADS // 2):
        sl = slice(pair * LANES, (pair + 1) * LANES)
        q2 = q_ref[:, sl] * ATTN_SCALE
        s_c = jnp.where(mask_c, _dot_nt(q2, two_heads(kc_ref[:, sl])), -jnp.inf)
        s_p = jnp.where(mask_p, _dot_nt(q2, two_heads(kp_ref[:, sl])), -jnp.inf)
        s_max = jnp.maximum(s_c, s_p)
        m0 = jnp.max(s_max[:, :BLK], axis=1, keepdims=True)
        m1 = jnp.max(s_max[:, BLK:], axis=1, keepdims=True)
        m = jnp.concatenate([jnp.broadcast_to(m0, (BLK, BLK)), jnp.broadcast_to(m1, (BLK, BLK))], axis=1)
        e_c = jnp.exp(s_c - m).astype(BF16)
        e_p = jnp.exp(s_p - m).astype(BF16)
        v_c = jnp.concatenate([two_heads(vc_ref[:, sl]), ones_sel], axis=1)
        v_p = jnp.concatenate([two_heads(vp_ref[:, sl]), ones_sel], axis=1)
        res = _dot(e_c, v_c) + _dot(e_p, v_p)
        den = res[:, LANES:]
        out = res[:, :LANES] / den
        if dil == 1:
            o_slab[pair, pl.ds(start, BLK), :] = out
        else:
            o_slab[pair, pl.ds(start, BLK, stride=dil), :] = out
        lse_tile = jnp.where(lane == 2 * pair, m0 + jnp.log(den[:, :1]), lse_tile)
        lse_tile = jnp.where(lane == 2 * pair + 1, m1 + jnp.log(den[:, HEAD_DIM:HEAD_DIM + 1]), lse_tile)
    if dil == 1:
        lse_slab[pl.ds(start, BLK), :] = lse_tile
    else:
        lse_slab[pl.ds(start, BLK, stride=dil), :] = lse_tile

    @pl.when(u == A_UNITS - 1)
    def _():
        for k in range(SLABS):
            o_ref[:, k * LANES:(k + 1) * LANES] = o_slab[k].astype(o_ref.dtype)
        lse_ref[...] = lse_slab[...]


def dil_attn(qkv, g, dil, tiles_per_seq):
    M = qkv.shape[2]
    per_stream = A_UNITS // dil

    def cur(part):
        return pl.BlockSpec((None, None, BLK, INNER), lambda c, u: (g, part, c * A_UNITS + u, 0))

    def prev(part):
        def index(c, u):
            blk = c * A_UNITS + u
            t = u % per_stream
            in_tile = blk - 1
            in_prev_tile = blk - A_UNITS + per_stream - 1
            has_prev_tile = c % tiles_per_seq > 0
            return (g, part, jnp.where(t > 0, in_tile, jnp.where(has_prev_tile, in_prev_tile, blk)), 0)
        return pl.BlockSpec((None, None, BLK, INNER), index)

    return pl.pallas_call(
        functools.partial(_dil_attn_kernel, dil=dil, tiles_per_seq=tiles_per_seq),
        grid=(M // A_TILE, A_UNITS),
        in_specs=[cur(0), prev(1), cur(1), prev(2), cur(2)],
        out_specs=[pl.BlockSpec((A_TILE, INNER), lambda c, u: (c, 0)),
                   pl.BlockSpec((A_TILE, LANES), lambda c, u: (c, 0))],
        out_shape=[jax.ShapeDtypeStruct((M, INNER), BF16), jax.ShapeDtypeStruct((M, LANES), F32)],
        scratch_shapes=[pltpu.VMEM((SLABS, A_TILE, LANES), F32), pltpu.VMEM((A_TILE, LANES), F32)],
        compiler_params=_params("arbitrary", "arbitrary"),
        name=f"dil_attn_d{dil}",
    )(qkv, qkv, qkv, qkv, qkv)


def _dil_sample_kernel(*refs):
    new_refs = refs[:3 * A_GROUPS]
    cache_refs = refs[3 * A_GROUPS:5 * A_GROUPS]
    o_ref = refs[5 * A_GROUPS]
    heads = HEADS // SAMPLE_HEAD_GROUPS
    scores, new_scores = [], []
    for g, dil in enumerate(A_DILATIONS):
        q_col = _row_to_col(new_refs[g][...])
        k_col = _row_to_col(new_refs[A_GROUPS + g][...])
        kt = cache_refs[2 * g][...]
        s = _head_sum(kt * q_col, heads) * ATTN_SCALE
        pos = lax.broadcasted_iota(jnp.int32, s.shape, 1)
        scores.append(jnp.where((pos & (dil - 1)) == 0, s, -jnp.inf))
        new_scores.append(_head_sum(q_col * k_col, heads) * ATTN_SCALE)
    m = new_scores[0]
    for g in range(A_GROUPS):
        m = jnp.maximum(m, jnp.maximum(new_scores[g], jnp.max(scores[g], axis=1, keepdims=True)))
    num = jnp.zeros((heads * HEAD_DIM, 1), F32)
    den = jnp.zeros((heads, 1), F32)
    for g in range(A_GROUPS):
        v_col = _row_to_col(new_refs[2 * A_GROUPS + g][...])
        e = jnp.exp(scores[g] - m)
        e_new = jnp.exp(new_scores[g] - m)
        vt = cache_refs[2 * g + 1][...]
        num = num + jnp.sum(vt * _head_rows(e, heads), axis=1, keepdims=True) + _head_rows(e_new, heads) * v_col
        den = den + jnp.sum(e, axis=1, keepdims=True) + e_new
    o_ref[...] = _col_to_row(num / _head_rows(den, heads))


def dil_sample(qkv, caches, layer):
    DB, W = qkv.shape
    hgs = SAMPLE_HEAD_GROUPS
    width = INNER // hgs
    new_specs = [pl.BlockSpec((None, 1, width), lambda b, h, c=(part * A_GROUPS + g) * hgs: (b, 0, c + h))
                 for part in range(3) for g in range(A_GROUPS)]
    views, specs = [], []
    for cache, win, dil in zip(caches, A_WINDOWS, A_DILATIONS):
        assert cache.shape[2] == win and win // dil == BLK and dil & (dil - 1) == 0
        view = jnp.transpose(cache, (0, 1, 3, 4, 5, 2)).reshape(cache.shape[0], DB, 2 * INNER, win)
        views += [view, view]
        specs += [pl.BlockSpec((None, None, width, win), lambda b, h: (layer, b, h, 0)),
                  pl.BlockSpec((None, None, width, win), lambda b, h: (layer, b, hgs + h, 0))]
    qkv3 = qkv.reshape(DB, 1, W)
    o = pl.pallas_call(
        _dil_sample_kernel,
        grid=(DB, hgs),
        in_specs=new_specs + specs,
        out_specs=pl.BlockSpec((None, 1, width), lambda b, h: (b, 0, h)),
        out_shape=jax.ShapeDtypeStruct((DB, 1, INNER), F32),
        compiler_params=_params("arbitrary", "arbitrary"),
        name="dil_sample",
    )(*([qkv3] * (3 * A_GROUPS)), *views)
    return o.reshape(DB, INNER)


def _sgu_front(x, g_ref, win_ref, lng_ref, lnb_ref):
    z = jax.nn.gelu(_dot(_rms(x, g_ref[...]).astype(BF16), win_ref[...]))
    width = z.shape[1] // 2
    u, vr = z[:, :width], z[:, width:]
    mu = jnp.mean(vr, axis=-1, keepdims=True)
    var = jnp.mean(jnp.square(vr - mu), axis=-1, keepdims=True)
    return u, (vr - mu) * lax.rsqrt(var + EPS) * lng_ref[...] + lnb_ref[...]


def _sgu_prompt_kernel(x_ref, g_ref, win_ref, lng_ref, lnb_ref, ws_ref, bs_ref, wout_ref, y_ref, us_ref):
    x = x_ref[...]
    u, v = _sgu_front(x, g_ref, win_ref, lng_ref, lnb_ref)
    causal = (lax.broadcasted_iota(jnp.int32, (CHUNK, CHUNK), 0)
              >= lax.broadcasted_iota(jnp.int32, (CHUNK, CHUNK), 1))
    for grp in range(B_GROUPS):
        cols = slice(grp * CHUNK, (grp + 1) * CHUNK)
        wm = jnp.where(causal, ws_ref[grp], 0.0).astype(BF16)
        for c in range(x.shape[0] // CHUNK):
            rows = slice(c * CHUNK, (c + 1) * CHUNK)
            s = _dot(wm, v[rows, cols].astype(BF16)) + bs_ref[:, cols]
            us_ref[rows, cols] = (u[rows, cols] * s).astype(BF16)
    y_ref[...] = x + _dot(us_ref[...], wout_ref[...])


def sgu_prompt(x, g, w_in, ln_g, ln_b, w_s, b_s, w_out, tm):
    M, D = x.shape
    width = w_out.shape[0]
    bias = jnp.repeat(b_s.T, width // B_GROUPS, axis=1)
    const = lambda shape: pl.BlockSpec(shape, lambda i: (0,) * len(shape))
    return pl.pallas_call(
        _sgu_prompt_kernel,
        grid=(M // tm,),
        in_specs=[pl.BlockSpec((tm, D), lambda i: (i, 0)), const((1, D)), const((D, 2 * width)),
                  const((1, width)), const((1, width)), const((B_GROUPS, CHUNK, CHUNK)),
                  const((CHUNK, width)), const((width, D))],
        out_specs=pl.BlockSpec((tm, D), lambda i: (i, 0)),
        out_shape=jax.ShapeDtypeStruct((M, D), F32),
        scratch_shapes=[pltpu.VMEM((tm, width), BF16)],
        compiler_params=_params("arbitrary"),
        name="sgu_prompt",
    )(x, g.reshape(1, D), w_in, ln_g.reshape(1, width), ln_b.reshape(1, width), w_s, bias, w_out)


def _sgu_sample_kernel(x_ref, g_ref, win_ref, lng_ref, lnb_ref, sw_ref, sb_ref, wout_ref, y_ref, v_ref):
    x = x_ref[...]
    u, v = _sgu_front(x, g_ref, win_ref, lng_ref, lnb_ref)
    s = sw_ref[...] * v + sb_ref[...]
    y_ref[...] = x + _dot((u * s).astype(BF16), wout_ref[...])
    v_ref[...] = v


def sgu_sample(x, g, w_in, ln_g, ln_b, w_s, b_s, w_out):
    M, D = x.shape
    width = w_out.shape[0]
    sw = jnp.repeat(w_s[:, 0, 0], width // B_GROUPS).reshape(1, width)
    sb = jnp.repeat(b_s[:, 0], width // B_GROUPS).reshape(1, width)
    return pl.pallas_call(
        _sgu_sample_kernel,
        out_shape=[jax.ShapeDtypeStruct((M, D), F32), jax.ShapeDtypeStruct((M, width), F32)],
        compiler_params=pltpu.CompilerParams(vmem_limit_bytes=VMEM_LIMIT),
        name="sgu_sample",
    )(x, g.reshape(1, D), w_in, ln_g.reshape(1, width), ln_b.reshape(1, width), sw, sb, w_out)


def _sb_attn_kernel(q_ref, k_ref, v_ref, o_ref):
    i = pl.program_id(2)
    nq, nk = SB_QBLK, SB_KBLK
    lane = lax.broadcasted_iota(jnp.int32, (nq, LANES), 1)
    first_head = lane < HEAD_DIM
    later = _later(nk)
    q2 = q_ref[...] * ATTN_SCALE
    zero = jnp.zeros_like(q2)
    qs = jnp.concatenate([jnp.where(first_head, q2, zero), jnp.where(first_head, zero, q2)], axis=0)

    def rows_from(x, lo):
        return x if lo == 0 else jnp.concatenate([x[lo:nq], x[nq + lo:]], axis=0)

    def block(kb, o, c, lo):
        start = pl.multiple_of(kb * nk, nk)
        first = 0 if lo is None else lo
        n = nq - first
        z = _dot_nt(rows_from(qs, first), k_ref[pl.ds(start, nk), :])
        sp = _softplus(z)
        if lo is not None:
            r = lax.broadcasted_iota(jnp.int32, (2 * n, nk), 0)
            valid = lax.broadcasted_iota(jnp.int32, (2 * n, nk), 1) < jnp.where(r < n, r, r - n)
            sp = jnp.where(valid, sp, 0.0)
        cs = _dot(sp.astype(BF16), later)
        a = jnp.exp(z - sp - cs - rows_from(c, first))
        if lo is not None:
            a = jnp.where(valid, a, 0.0)
        do = _dot(a.astype(BF16), v_ref[pl.ds(start, nk), :])
        dc = cs[:, :1] + sp[:, :1]
        if first == 0:
            return o + do, c + dc
        pad = lambda d: jnp.concatenate(
            [jnp.zeros((first, d.shape[1]), F32), d[:n], jnp.zeros((first, d.shape[1]), F32), d[n:]], axis=0)
        return o + pad(do), c + pad(dc)

    o = jnp.zeros((2 * nq, LANES), F32)
    c = jnp.zeros((2 * nq, 1), F32)
    first_kb = i * (nq // nk)
    for d in reversed(range(nq // nk)):
        o, c = block(first_kb + d, o, c, d * nk)
    o, c = lax.fori_loop(0, first_kb, lambda s, oc: block(first_kb - 1 - s, oc[0], oc[1], None), (o, c))
    o_ref[...] = jnp.where(first_head, o[:nq], o[nq:]).astype(o_ref.dtype)


def sb_attn(qkv):
    B, S, _ = qkv.shape
    pairs = INNER // LANES
    return pl.pallas_call(
        _sb_attn_kernel,
        grid=(B, pairs, S // SB_QBLK),
        in_specs=[pl.BlockSpec((None, SB_QBLK, LANES), lambda b, p, i: (b, i, p)),
                  pl.BlockSpec((None, S, LANES), lambda b, p, i: (b, 0, pairs + p)),
                  pl.BlockSpec((None, S, LANES), lambda b, p, i: (b, 0, 2 * pairs + p))],
        out_specs=pl.BlockSpec((None, SB_QBLK, LANES), lambda b, p, i: (b, i, p)),
        out_shape=jax.ShapeDtypeStruct((B, S, INNER), BF16),
        compiler_params=_params("arbitrary", "arbitrary", "arbitrary"),
        name="sb_attn",
    )(qkv, qkv, qkv)


def _sb_sample_kernel(pt_ref, q_ref, *refs):
    del pt_ref
    page_refs = refs[:SB_PAGES_PER_STEP]
    o_ref, qcol_ref, acc_ref, c_ref = refs[SB_PAGES_PER_STEP:]
    p = pl.program_id(1)
    n = page_refs[0].shape[1]

    @pl.when(p == 0)
    def _():
        for k in range(SLABS):
            rows = slice(k * LANES, (k + 1) * LANES)
            qcol_ref[rows, :] = jnp.broadcast_to(_row_to_col(q_ref[:, rows]), (LANES, n))
        acc_ref[...] = jnp.zeros_like(acc_ref)
        c_ref[...] = jnp.zeros_like(c_ref)

    later = _later(n)
    for page_ref in page_refs:
        z = _head_sum(page_ref[:INNER, :] * qcol_ref[...], HEADS) * ATTN_SCALE
        sp = _softplus(z)
        a = jnp.exp(z - sp - _dot_split(sp, later) - c_ref[...])
        acc_ref[...] += page_ref[INNER:, :] * _head_rows(a, HEADS)
        c_ref[...] += jnp.sum(sp, axis=1, keepdims=True)

    @pl.when(p == pl.num_programs(1) - 1)
    def _():
        for k in range(SLABS):
            rows = slice(k * LANES, (k + 1) * LANES)
            o_ref[:, rows] = _col_to_row(jnp.sum(acc_ref[rows, :], axis=1, keepdims=True))


def sb_sample(q, pool, layer, page_table):
    DB = q.shape[0]
    layers, n_pool, page = pool.shape[:3]
    n_pages = page_table.shape[1]
    pps = SB_PAGES_PER_STEP
    assert n_pages % pps == 0
    pool_t = jnp.transpose(pool, (0, 1, 3, 4, 5, 2)).reshape(layers, n_pool, 2 * INNER, page)

    def page_spec(k):
        return pl.BlockSpec((None, None, 2 * INNER, page),
                            lambda b, p, pt: (layer, pt[b, n_pages - 1 - (p * pps + k)], 0, 0))

    o = pl.pallas_call(
        _sb_sample_kernel,
        grid_spec=pltpu.PrefetchScalarGridSpec(
            num_scalar_prefetch=1,
            grid=(DB, n_pages // pps),
            in_specs=[pl.BlockSpec((None, 1, INNER), lambda b, p, pt: (b, 0, 0))]
            + [page_spec(k) for k in range(pps)],
            out_specs=pl.BlockSpec((None, 1, INNER), lambda b, p, pt: (b, 0, 0)),
            scratch_shapes=[pltpu.VMEM((INNER, page), F32), pltpu.VMEM((INNER, page), F32),
                            pltpu.VMEM((HEADS, page), F32)]),
        out_shape=jax.ShapeDtypeStruct((DB, 1, INNER), F32),
        compiler_params=_params("arbitrary", "arbitrary"),
        name="sb_sample",
    )(page_table, q.reshape(DB, 1, INNER), *([pool_t] * pps))
    return o.reshape(DB, INNER)


def _ffn_ple_kernel(x_ref, p_ref, gf_ref, w1_ref, w2_ref, gp_ref, wproj_ref, wgate_ref, *rest, final):
    if final:
        gfin_ref, y_ref, h_ref, acc_ref = rest
    else:
        y_ref, h_ref, acc_ref = rest
    x = x_ref[...]
    h_ref[...] = _rms(x, gf_ref[...]).astype(BF16)
    hidden = w1_ref.shape[1]
    for c in range(hidden // FFN_CHUNK):
        cols = slice(c * FFN_CHUNK, (c + 1) * FFN_CHUNK)
        a = jnp.square(jnp.maximum(_dot(h_ref[...], w1_ref[:, cols]), 0.0))
        part = _dot(a.astype(BF16), w2_ref[cols, :])
        if c == 0:
            acc_ref[...] = part
        else:
            acc_ref[...] += part
    x2 = x + acc_ref[...]
    gate = jax.nn.sigmoid(_dot(_rms(x2, gp_ref[...]).astype(BF16), wgate_ref[...]))
    y = x2 + _dot(p_ref[...].astype(BF16), wproj_ref[...]) * gate
    if final:
        y = _rms(y, gfin_ref[...])
    y_ref[...] = y


def ffn_ple(x, layer, p, g_ffn, w1, w2, g_ple, w_proj, w_gate, g_final, tm):
    M, D = x.shape
    F = w1.shape[2]
    P = p.shape[2]
    const = lambda shape: pl.BlockSpec(shape, lambda i: (0, 0))
    in_specs = [pl.BlockSpec((tm, D), lambda i: (i, 0)),
                pl.BlockSpec((None, tm, P), lambda i: (layer, i, 0)),
                const((1, D)),
                pl.BlockSpec((None, D, F), lambda i: (layer, 0, 0)),
                pl.BlockSpec((None, F, D), lambda i: (layer, 0, 0)),
                const((1, D)),
                pl.BlockSpec((None, P, D), lambda i: (layer, 0, 0)),
                pl.BlockSpec((None, D, D), lambda i: (layer, 0, 0))]
    args = [x, p, g_ffn.reshape(1, D), w1, w2, g_ple.reshape(1, D), w_proj, w_gate]
    if g_final is not None:
        in_specs.append(const((1, D)))
        args.append(g_final.reshape(1, D))
    return pl.pallas_call(
        functools.partial(_ffn_ple_kernel, final=g_final is not None),
        grid=(M // tm,),
        in_specs=in_specs,
        out_specs=pl.BlockSpec((tm, D), lambda i: (i, 0)),
        out_shape=jax.ShapeDtypeStruct((M, D), F32),
        scratch_shapes=[pltpu.VMEM((tm, D), BF16), pltpu.VMEM((tm, D), F32)],
        compiler_params=_params("arbitrary"),
        name="ffn_ple",
    )(*args)


def kernel(x_prompt, x_sample, p_prompt, p_sample, cache_a_w128, cache_a_w512, cache_a_w2048, cache_c_kv, page_table, norm_mix, norm_ffn, norm_ple, norm_final, a_w_in, a_w_out, b_w_in, b_ln_g, b_ln_b, b_w_s, b_b_s, b_w_out, c_w_in, c_b_in, c_w_out, ffn_w1, ffn_w2, ple_w_proj, ple_w_gate):
    B, S, D = x_prompt.shape
    DB, T, _ = x_sample.shape
    assert T == 1, "the sample path handles one new token per sequence"
    assert S % A_TILE == 0 and S % SB_QBLK == 0
    depth = norm_mix.shape[0]
    MP, MS = B * S, DB * T
    tm = min(512, MP)
    bf = lambda w: w.astype(BF16)

    xp = x_prompt.reshape(MP, D)
    xs = x_sample.reshape(MS, D)
    caches_a = (cache_a_w128, cache_a_w512, cache_a_w2048)
    w1_all, w2_all, w_proj_all, w_gate_all = bf(ffn_w1), bf(ffn_w2), bf(ple_w_proj), bf(ple_w_gate)
    a_w_in_all = bf(a_w_in)
    a_p = [[] for _ in A_WINDOWS]
    a_s = [[] for _ in A_WINDOWS]
    b_s_rows, c_p, c_s = [], [], []

    for i in range(depth):
        kind, j = i % N_MIXERS, i // N_MIXERS
        if kind == 0:
            w_in, w_out = a_w_in_all[j], bf(a_w_out[j])
            w4 = w_in.reshape(D, 3, A_GROUPS, INNER)
            qkv_p = group_matmul(norm_streams(xp, norm_mix[i]), a_w_in_all, j, 1024)
            outs, lses = zip(*[dil_attn(qkv_p, g, dil, S // A_TILE) for g, dil in enumerate(A_DILATIONS)])
            x3 = xp.reshape(B, S, D)
            for g, win in enumerate(A_WINDOWS):
                wt_kv = jnp.transpose(w4[:, 1:, g, :].reshape(D, 2 * INNER))
                a_p[g].append(kv_rows_from_t(norm_matmul_t(x3, norm_mix[i], wt_kv, None, min(win, S), 2 * INNER)))
            xp = merge_outproj(xp, outs, lses, w_out, tm)
            qkv_s = norm_matmul(xs, norm_mix[i], w_in, None, F32, MS, 1024)
            xs = outproj(xs, dil_sample(qkv_s, caches_a, j), w_out, MS)
            kv_s = qkv_s.reshape(DB, T, 3, A_GROUPS, HEADS, HEAD_DIM)
            for g in range(A_GROUPS):
                a_s[g].append(jnp.stack([kv_s[:, :, 1, g], kv_s[:, :, 2, g]], axis=2))
        elif kind == 1:
            w_in, w_out = bf(b_w_in[j]), bf(b_w_out[j])
            xp = sgu_prompt(xp, norm_mix[i], w_in, b_ln_g[j], b_ln_b[j], b_w_s[j], b_b_s[j], w_out, min(256, MP))
            xs, v_s = sgu_sample(xs, norm_mix[i], w_in, b_ln_g[j], b_ln_b[j], b_w_s[j], b_b_s[j], w_out)
            b_s_rows.append(v_s.reshape(DB, T, -1))
        else:
            w_in, w_out = bf(c_w_in[j]), bf(c_w_out[j])
            qkv_p = norm_matmul(xp, norm_mix[i], w_in, c_b_in[j], BF16, tm, 1024)
            kv_t = norm_matmul_t(xp.reshape(B, S, D), norm_mix[i], jnp.transpose(w_in[:, INNER:]),
                                 c_b_in[j][INNER:], S, 2 * INNER)
            c_p.append(kv_rows_from_t(kv_t))
            qkv_s = norm_matmul(xs, norm_mix[i], w_in, c_b_in[j], F32, MS, 1024)
            o_p = sb_attn(qkv_p.reshape(B, S, -1))
            xp = outproj(xp, o_p.reshape(MP, INNER), w_out, tm)
            xs = outproj(xs, sb_sample(qkv_s[:, :INNER], cache_c_kv, j, page_table), w_out, MS)
            c_s.append(qkv_s[:, INNER:].reshape(DB, T, 2, HEADS, HEAD_DIM))
        g_final = norm_final if i == depth - 1 else None
        tail = (norm_ffn[i], w1_all, w2_all, norm_ple[i], w_proj_all, w_gate_all, g_final)
        xp = ffn_ple(xp, i, p_prompt.reshape(depth, MP, -1), *tail, tm)
        xs = ffn_ple(xs, i, p_sample.reshape(depth, MS, -1), *tail, MS)

    return (xp.reshape(B, S, D), xs.reshape(DB, T, D),
            jnp.stack(a_p[0]), jnp.stack(a_p[1]), jnp.stack(a_p[2]),
            jnp.stack(a_s[0]), jnp.stack(a_s[1]), jnp.stack(a_s[2]),
            jnp.stack(b_s_rows), jnp.stack(c_p), jnp.stack(c_s))
```

```python
import functools
import math

import jax
import jax.numpy as jnp
from jax import lax
from jax.experimental import pallas as pl
from jax.experimental.pallas import tpu as pltpu

F32 = jnp.float32
BF16 = jnp.bfloat16

HEAD_DIM = 64
HEADS = 16
INNER = HEADS * HEAD_DIM
ATTN_SCALE = 1.0 / math.sqrt(HEAD_DIM)
LOG2E = 1.4426950408889634
BLK = 128
A_WINDOWS = (128, 512, 2048)
A_DILATIONS = (1, 4, 16)
A_GROUPS = 3
A_TILE = 2048
A_UNITS = A_TILE // BLK
B_GROUPS = 8
CHUNK = 128
SB_QBLK = 1024
SB_KBLK = 256
SB_PAGES_PER_STEP = 8
FFN_CHUNK = 512
SAMPLE_HEAD_GROUPS = 4
N_MIXERS = 3
EPS = 1e-6
LANES = 128
SLABS = INNER // LANES
VMEM_LIMIT = 56 * 1024 * 1024


def _params(*sem):
    return pltpu.CompilerParams(dimension_semantics=sem, vmem_limit_bytes=VMEM_LIMIT)


def _rms(xf, g):
    return xf * lax.rsqrt(jnp.mean(xf * xf, axis=-1, keepdims=True) + EPS) * g


def _dot(a, b):
    return jnp.dot(a, b, preferred_element_type=F32)


def _dot_nt(a, b):
    return lax.dot_general(a, b, (((1,), (1,)), ((), ())), preferred_element_type=F32)


def _dot_split(a, b_bf16):
    rows = a.shape[0]
    hi = a.astype(BF16)
    lo = (a - hi.astype(F32)).astype(BF16)
    both = _dot(jnp.concatenate([hi, lo], axis=0), b_bf16)
    return both[:rows] + both[rows:]


def _softplus(z):
    return jnp.maximum(z, 0.0) + jnp.log(1.0 + jnp.exp2(jnp.abs(z) * (-LOG2E)))


def _later(n):
    return (lax.broadcasted_iota(jnp.int32, (n, n), 0) > lax.broadcasted_iota(jnp.int32, (n, n), 1)).astype(BF16)


def _eye(n):
    return lax.broadcasted_iota(jnp.int32, (n, n), 0) == lax.broadcasted_iota(jnp.int32, (n, n), 1)


def _row_to_col(row):
    n = row.shape[1]
    return jnp.sum(jnp.where(_eye(n), row, 0.0), axis=1, keepdims=True)


def _col_to_row(col):
    n = col.shape[0]
    return jnp.sum(jnp.where(_eye(n), col, 0.0), axis=0, keepdims=True)


def _head_sum(x, heads):
    return jnp.sum(x.reshape(heads, HEAD_DIM, x.shape[1]), axis=1)


def _head_rows(x, heads):
    return jnp.broadcast_to(x[:, None, :], (heads, HEAD_DIM, x.shape[1])).reshape(heads * HEAD_DIM, x.shape[1])


def _norm_matmul_kernel(x_ref, g_ref, w_ref, *rest, has_bias):
    if has_bias:
        b_ref, o_ref, h_ref = rest
    else:
        o_ref, h_ref = rest

    @pl.when(pl.program_id(1) == 0)
    def _():
        h_ref[...] = _rms(x_ref[...], g_ref[...]).astype(BF16)

    acc = _dot(h_ref[...], w_ref[...])
    if has_bias:
        acc = acc + b_ref[...]
    o_ref[...] = acc.astype(o_ref.dtype)


def norm_matmul(x, g, w, bias, out_dtype, tm, tn):
    M, D = x.shape
    N = w.shape[1]
    in_specs = [pl.BlockSpec((tm, D), lambda i, j: (i, 0)),
                pl.BlockSpec((1, D), lambda i, j: (0, 0)),
                pl.BlockSpec((D, tn), lambda i, j: (0, j))]
    args = [x, g.reshape(1, D), w]
    if bias is not None:
        in_specs.append(pl.BlockSpec((1, tn), lambda i, j: (0, j)))
        args.append(bias.reshape(1, N))
    return pl.pallas_call(
        functools.partial(_norm_matmul_kernel, has_bias=bias is not None),
        grid=(M // tm, N // tn),
        in_specs=in_specs,
        out_specs=pl.BlockSpec((tm, tn), lambda i, j: (i, j)),
        out_shape=jax.ShapeDtypeStruct((M, N), out_dtype),
        scratch_shapes=[pltpu.VMEM((tm, D), BF16)],
        compiler_params=_params("arbitrary", "arbitrary"),
        name="norm_matmul",
    )(*args)


def _norm_matmul_t_kernel(*refs, n_layers, has_bias):
    x_refs = refs[:n_layers]
    g_ref, wt_ref = refs[n_layers:n_layers + 2]
    if has_bias:
        b_ref, o_ref = refs[n_layers + 2:]
    else:
        (o_ref,) = refs[n_layers + 2:]
    for layer, x_ref in enumerate(x_refs):
        @pl.when(pl.program_id(0) == layer)
        def _(x_ref=x_ref):
            h = _rms(x_ref[...], g_ref[...]).astype(BF16)
            acc = _dot_nt(wt_ref[...], h)
            if has_bias:
                acc = acc + b_ref[...]
            o_ref[...] = acc


def norm_matmul_t(xs, g, wt, bias, keep):
    n_layers = len(xs)
    B, S, D = xs[0].shape
    N = wt.shape[1]
    tm = min(512, keep)
    first = (S - keep) // tm
    in_specs = [pl.BlockSpec((None, tm, D), lambda l, b, i: (b, first + i, 0))] * n_layers
    in_specs += [pl.BlockSpec((None, 1, D), lambda l, b, i: (l, 0, 0)),
                 pl.BlockSpec((None, N, D), lambda l, b, i: (l, 0, 0))]
    args = list(xs) + [g.reshape(n_layers, 1, D), wt]
    if bias is not None:
        in_specs.append(pl.BlockSpec((None, N, 1), lambda l, b, i: (l, 0, 0)))
        args.append(bias.reshape(n_layers, N, 1))
    return pl.pallas_call(
        functools.partial(_norm_matmul_t_kernel, n_layers=n_layers, has_bias=bias is not None),
        grid=(n_layers, B, keep // tm),
        in_specs=in_specs,
        out_specs=pl.BlockSpec((None, None, N, tm), lambda l, b, i: (l, b, 0, i)),
        out_shape=jax.ShapeDtypeStruct((n_layers, B, N, keep), F32),
        compiler_params=_params("arbitrary", "arbitrary", "arbitrary"),
        name="norm_matmul_t",
    )(*args)


def kv_rows_from_t(kv_t):
    L, B, _, keep = kv_t.shape
    return jnp.transpose(kv_t.reshape(L, B, 2, HEADS, HEAD_DIM, keep), (0, 1, 5, 2, 3, 4))


def _outproj_kernel(x_ref, o_ref, w_ref, y_ref):
    y_ref[...] = x_ref[...] + _dot(o_ref[...].astype(BF16), w_ref[...])


def outproj(x, o, w, tm):
    M, D = x.shape
    K = o.shape[1]
    return pl.pallas_call(
        _outproj_kernel,
        grid=(M // tm,),
        in_specs=[pl.BlockSpec((tm, D), lambda i: (i, 0)),
                  pl.BlockSpec((tm, K), lambda i: (i, 0)),
                  pl.BlockSpec((K, D), lambda i: (0, 0))],
        out_specs=pl.BlockSpec((tm, D), lambda i: (i, 0)),
        out_shape=jax.ShapeDtypeStruct((M, D), F32),
        compiler_params=_params("arbitrary"),
        name="outproj",
    )(x, o, w)


def _merge_outproj_kernel(x_ref, o0_ref, o1_ref, o2_ref, l0_ref, l1_ref, l2_ref, w_ref, y_ref):
    l0, l1, l2 = l0_ref[...], l1_ref[...], l2_ref[...]
    m = jnp.maximum(jnp.maximum(l0, l1), l2)
    e0, e1, e2 = jnp.exp(l0 - m), jnp.exp(l1 - m), jnp.exp(l2 - m)
    den = e0 + e1 + e2
    expand = (lax.broadcasted_iota(jnp.int32, (LANES, INNER), 1) // HEAD_DIM
              == lax.broadcasted_iota(jnp.int32, (LANES, INNER), 0)).astype(BF16)
    inv = 1.0 / den
    o = (_dot((e0 * inv).astype(BF16), expand) * o0_ref[...].astype(F32)
         + _dot((e1 * inv).astype(BF16), expand) * o1_ref[...].astype(F32)
         + _dot((e2 * inv).astype(BF16), expand) * o2_ref[...].astype(F32))
    y_ref[...] = x_ref[...] + _dot(o.astype(BF16), w_ref[...])


def merge_outproj(x, outs, lses, w, tm):
    M, D = x.shape
    row = lambda width: pl.BlockSpec((tm, width), lambda i: (i, 0))
    return pl.pallas_call(
        _merge_outproj_kernel,
        grid=(M // tm,),
        in_specs=[row(D)] + [row(INNER)] * A_GROUPS + [row(LANES)] * A_GROUPS
        + [pl.BlockSpec((INNER, D), lambda i: (0, 0))],
        out_specs=row(D),
        out_shape=jax.ShapeDtypeStruct((M, D), F32),
        compiler_params=_params("arbitrary"),
        name="merge_outproj",
    )(x, *outs, *lses, w)


def _norm_streams_kernel(x_ref, g_ref, h_ref, slab_ref):
    grp = pl.program_id(1)

    @pl.when(grp == 0)
    def _():
        rows = 256
        for c in range(A_TILE // rows):
            sl = slice(c * rows, (c + 1) * rows)
            h = _rms(x_ref[sl, :], g_ref[...])
            h_ref[sl, :] = h.astype(BF16)
            for k in range(SLABS):
                slab_ref[k, sl, :] = h[:, k * LANES:(k + 1) * LANES]

    for g, dil in enumerate(A_DILATIONS):
        if dil == 1:
            continue
        per_stream = A_TILE // dil

        @pl.when(grp == g)
        def _(dil=dil, per_stream=per_stream):
            for r in range(dil):
                for k in range(SLABS):
                    h_ref[r * per_stream:(r + 1) * per_stream, k * LANES:(k + 1) * LANES] = (
                        slab_ref[k, pl.ds(r, per_stream, stride=dil), :].astype(BF16))


def norm_streams(x, g):
    M, D = x.shape
    assert A_DILATIONS[0] == 1
    return pl.pallas_call(
        _norm_streams_kernel,
        grid=(M // A_TILE, A_GROUPS),
        in_specs=[pl.BlockSpec((A_TILE, D), lambda i, g: (i, 0)), pl.BlockSpec((1, D), lambda i, g: (0, 0))],
        out_specs=pl.BlockSpec((None, A_TILE, D), lambda i, g: (g, i, 0)),
        out_shape=jax.ShapeDtypeStruct((A_GROUPS, M, D), BF16),
        scratch_shapes=[pltpu.VMEM((SLABS, A_TILE, LANES), F32)],
        compiler_params=_params("arbitrary", "arbitrary"),
        name="norm_streams",
    )(x, g.reshape(1, D))


def _group_matmul_kernel(h_ref, w_ref, o_ref):
    o_ref[...] = _dot(h_ref[...], w_ref[...]).astype(o_ref.dtype)


def group_matmul(h, w, layer, tm):
    G, M, D = h.shape
    P = w.shape[2] // (G * INNER)
    return pl.pallas_call(
        _group_matmul_kernel,
        grid=(G, M // tm, P),
        in_specs=[pl.BlockSpec((None, tm, D), lambda g, i, j: (g, i, 0)),
                  pl.BlockSpec((None, D, INNER), lambda g, i, j: (layer, 0, j * G + g))],
        out_specs=pl.BlockSpec((None, None, tm, INNER), lambda g, i, j: (g, j, i, 0)),
        out_shape=jax.ShapeDtypeStruct((G, P, M, INNER), BF16),
        compiler_params=_params("arbitrary", "arbitrary", "arbitrary"),
        name="group_matmul",
    )(h, w)


def _dil_attn_kernel(q_ref, kp_ref, kc_ref, vp_ref, vc_ref, o_ref, lse_ref, o_slab, lse_slab, *,
                     dil, tiles_per_seq):
    per_stream = A_UNITS // dil
    u = pl.program_id(1)
    r, t = u // per_stream, u % per_stream
    has_prev = (t > 0) | (pl.program_id(0) % tiles_per_seq > 0)
    row = lax.broadcasted_iota(jnp.int32, (BLK, 2 * BLK), 0)
    col = lax.broadcasted_iota(jnp.int32, (BLK, 2 * BLK), 1) & (BLK - 1)
    mask_c = col <= row
    mask_p = (col >= row) & has_prev
    lane = lax.broadcasted_iota(jnp.int32, (BLK, LANES), 1)
    first_head = lane < HEAD_DIM
    rows2 = lax.broadcasted_iota(jnp.int32, (2 * BLK, LANES), 0)
    lanes2 = lax.broadcasted_iota(jnp.int32, (2 * BLK, LANES), 1)
    ones_sel = ((rows2 < BLK) == (lanes2 < HEAD_DIM)).astype(BF16)
    lse_tile = jnp.zeros((BLK, LANES), F32)
    start = r + dil * BLK * t

    def two_heads(x):
        zero = jnp.zeros_like(x)
        return jnp.concatenate([jnp.where(first_head, x, zero), jnp.where(first_head, zero, x)], axis=0)

    for pair in range(HEADS // 2):
        sl = slice(pair * LANES, (pair + 1) * LANES)
        q2 = q_ref[:, sl] * ATTN_SCALE
        s_c = jnp.where(mask_c, _dot_nt(q2, two_heads(kc_ref[:, sl])), -jnp.inf)
        s_p = jnp.where(mask_p, _dot_nt(q2, two_heads(kp_ref[:, sl])), -jnp.inf)
        s_max = jnp.maximum(s_c, s_p)
        m0 = jnp.max(s_max[:, :BLK], axis=1, keepdims=True)
        m1 = jnp.max(s_max[:, BLK:], axis=1, keepdims=True)
        m = jnp.concatenate([jnp.broadcast_to(m0, (BLK, BLK)), jnp.broadcast_to(m1, (BLK, BLK))], axis=1)
        e_c = jnp.exp(s_c - m).astype(BF16)
        e_p = jnp.exp(s_p - m).astype(BF16)
        v_c = jnp.concatenate([two_heads(vc_ref[:, sl]), ones_sel], axis=1)
        v_p = jnp.concatenate([two_heads(vp_ref[:, sl]), ones_sel], axis=1)
        res = _dot(e_c, v_c) + _dot(e_p, v_p)
        den = res[:, LANES:]
        out = res[:, :LANES] / den
        if dil == 1:
            o_slab[pair, pl.ds(start, BLK), :] = out
        else:
            o_slab[pair, pl.ds(start, BLK, stride=dil), :] = out
        lse_tile = jnp.where(lane == 2 * pair, m0 + jnp.log(den[:, :1]), lse_tile)
        lse_tile = jnp.where(lane == 2 * pair + 1, m1 + jnp.log(den[:, HEAD_DIM:HEAD_DIM + 1]), lse_tile)
    if dil == 1:
        lse_slab[pl.ds(start, BLK), :] = lse_tile
    else:
        lse_slab[pl.ds(start, BLK, stride=dil), :] = lse_tile

    @pl.when(u == A_UNITS - 1)
    def _():
        for k in range(SLABS):
            o_ref[:, k * LANES:(k + 1) * LANES] = o_slab[k].astype(o_ref.dtype)
        lse_ref[...] = lse_slab[...]


def dil_attn(qkv, g, dil, tiles_per_seq):
    M = qkv.shape[2]
    per_stream = A_UNITS // dil

    def cur(part):
        return pl.BlockSpec((None, None, BLK, INNER), lambda c, u: (g, part, c * A_UNITS + u, 0))

    def prev(part):
        def index(c, u):
            blk = c * A_UNITS + u
            t = u % per_stream
            in_tile = blk - 1
            in_prev_tile = blk - A_UNITS + per_stream - 1
            has_prev_tile = c % tiles_per_seq > 0
            return (g, part, jnp.where(t > 0, in_tile, jnp.where(has_prev_tile, in_prev_tile, blk)), 0)
        return pl.BlockSpec((None, None, BLK, INNER), index)

    return pl.pallas_call(
        functools.partial(_dil_attn_kernel, dil=dil, tiles_per_seq=tiles_per_seq),
        grid=(M // A_TILE, A_UNITS),
        in_specs=[cur(0), prev(1), cur(1), prev(2), cur(2)],
        out_specs=[pl.BlockSpec((A_TILE, INNER), lambda c, u: (c, 0)),
                   pl.BlockSpec((A_TILE, LANES), lambda c, u: (c, 0))],
        out_shape=[jax.ShapeDtypeStruct((M, INNER), BF16), jax.ShapeDtypeStruct((M, LANES), F32)],
        scratch_shapes=[pltpu.VMEM((SLABS, A_TILE, LANES), F32), pltpu.VMEM((A_TILE, LANES), F32)],
        compiler_params=_params("arbitrary", "arbitrary"),
        name=f"dil_attn_d{dil}",
    )(qkv, qkv, qkv, qkv, qkv)


def _dil_sample_kernel(*refs):
    new_refs = refs[:3 * A_GROUPS]
    cache_refs = refs[3 * A_GROUPS:5 * A_GROUPS]
    o_ref = refs[5 * A_GROUPS]
    heads = HEADS // SAMPLE_HEAD_GROUPS
    scores, new_scores = [], []
    for g, dil in enumerate(A_DILATIONS):
        q_col = _row_to_col(new_refs[g][...])
        k_col = _row_to_col(new_refs[A_GROUPS + g][...])
        kt = cache_refs[2 * g][...]
        s = _head_sum(kt * q_col, heads) * ATTN_SCALE
        pos = lax.broadcasted_iota(jnp.int32, s.shape, 1)
        scores.append(jnp.where((pos & (dil - 1)) == 0, s, -jnp.inf))
        new_scores.append(_head_sum(q_col * k_col, heads) * ATTN_SCALE)
    m = new_scores[0]
    for g in range(A_GROUPS):
        m = jnp.maximum(m, jnp.maximum(new_scores[g], jnp.max(scores[g], axis=1, keepdims=True)))
    num = jnp.zeros((heads * HEAD_DIM, 1), F32)
    den = jnp.zeros((heads, 1), F32)
    for g in range(A_GROUPS):
        v_col = _row_to_col(new_refs[2 * A_GROUPS + g][...])
        e = jnp.exp(scores[g] - m)
        e_new = jnp.exp(new_scores[g] - m)
        vt = cache_refs[2 * g + 1][...]
        num = num + jnp.sum(vt * _head_rows(e, heads), axis=1, keepdims=True) + _head_rows(e_new, heads) * v_col
        den = den + jnp.sum(e, axis=1, keepdims=True) + e_new
    o_ref[...] = _col_to_row(num / _head_rows(den, heads))


def dil_sample(qkv, caches, layer):
    DB, W = qkv.shape
    hgs = SAMPLE_HEAD_GROUPS
    width = INNER // hgs
    new_specs = [pl.BlockSpec((None, 1, width), lambda b, h, c=(part * A_GROUPS + g) * hgs: (b, 0, c + h))
                 for part in range(3) for g in range(A_GROUPS)]
    views, specs = [], []
    for cache, win, dil in zip(caches, A_WINDOWS, A_DILATIONS):
        assert cache.shape[2] == win and win // dil == BLK and dil & (dil - 1) == 0
        view = jnp.transpose(cache, (0, 1, 3, 4, 5, 2)).reshape(cache.shape[0], DB, 2 * INNER, win)
        views += [view, view]
        specs += [pl.BlockSpec((None, None, width, win), lambda b, h: (layer, b, h, 0)),
                  pl.BlockSpec((None, None, width, win), lambda b, h: (layer, b, hgs + h, 0))]
    qkv3 = qkv.reshape(DB, 1, W)
    o = pl.pallas_call(
        _dil_sample_kernel,
        grid=(DB, hgs),
        in_specs=new_specs + specs,
        out_specs=pl.BlockSpec((None, 1, width), lambda b, h: (b, 0, h)),
        out_shape=jax.ShapeDtypeStruct((DB, 1, INNER), F32),
        compiler_params=_params("arbitrary", "arbitrary"),
        name="dil_sample",
    )(*([qkv3] * (3 * A_GROUPS)), *views)
    return o.reshape(DB, INNER)


def _sgu_front(x, g_ref, win_ref, lng_ref, lnb_ref):
    z = jax.nn.gelu(_dot(_rms(x, g_ref[...]).astype(BF16), win_ref[...]))
    width = z.shape[1] // 2
    u, vr = z[:, :width], z[:, width:]
    mu = jnp.mean(vr, axis=-1, keepdims=True)
    var = jnp.mean(jnp.square(vr - mu), axis=-1, keepdims=True)
    return u, (vr - mu) * lax.rsqrt(var + EPS) * lng_ref[...] + lnb_ref[...]


def _sgu_prompt_kernel(x_ref, g_ref, win_ref, lng_ref, lnb_ref, ws_ref, bs_ref, wout_ref, y_ref, us_ref):
    x = x_ref[...]
    u, v = _sgu_front(x, g_ref, win_ref, lng_ref, lnb_ref)
    causal = (lax.broadcasted_iota(jnp.int32, (CHUNK, CHUNK), 0)
              >= lax.broadcasted_iota(jnp.int32, (CHUNK, CHUNK), 1))
    for grp in range(B_GROUPS):
        cols = slice(grp * CHUNK, (grp + 1) * CHUNK)
        wm = jnp.where(causal, ws_ref[grp], 0.0).astype(BF16)
        for c in range(x.shape[0] // CHUNK):
            rows = slice(c * CHUNK, (c + 1) * CHUNK)
            s = _dot(wm, v[rows, cols].astype(BF16)) + bs_ref[:, cols]
            us_ref[rows, cols] = (u[rows, cols] * s).astype(BF16)
    y_ref[...] = x + _dot(us_ref[...], wout_ref[...])


def sgu_prompt(x, g, w_in, ln_g, ln_b, w_s, b_s, w_out, tm):
    M, D = x.shape
    width = w_out.shape[0]
    bias = jnp.repeat(b_s.T, width // B_GROUPS, axis=1)
    const = lambda shape: pl.BlockSpec(shape, lambda i: (0,) * len(shape))
    return pl.pallas_call(
        _sgu_prompt_kernel,
        grid=(M // tm,),
        in_specs=[pl.BlockSpec((tm, D), lambda i: (i, 0)), const((1, D)), const((D, 2 * width)),
                  const((1, width)), const((1, width)), const((B_GROUPS, CHUNK, CHUNK)),
                  const((CHUNK, width)), const((width, D))],
        out_specs=pl.BlockSpec((tm, D), lambda i: (i, 0)),
        out_shape=jax.ShapeDtypeStruct((M, D), F32),
        scratch_shapes=[pltpu.VMEM((tm, width), BF16)],
        compiler_params=_params("arbitrary"),
        name="sgu_prompt",
    )(x, g.reshape(1, D), w_in, ln_g.reshape(1, width), ln_b.reshape(1, width), w_s, bias, w_out)


def _sgu_sample_kernel(x_ref, g_ref, win_ref, lng_ref, lnb_ref, sw_ref, sb_ref, wout_ref, y_ref, v_ref):
    x = x_ref[...]
    u, v = _sgu_front(x, g_ref, win_ref, lng_ref, lnb_ref)
    s = sw_ref[...] * v + sb_ref[...]
    y_ref[...] = x + _dot((u * s).astype(BF16), wout_ref[...])
    v_ref[...] = v


def sgu_sample(x, g, w_in, ln_g, ln_b, w_s, b_s, w_out):
    M, D = x.shape
    width = w_out.shape[0]
    sw = jnp.repeat(w_s[:, 0, 0], width // B_GROUPS).reshape(1, width)
    sb = jnp.repeat(b_s[:, 0], width // B_GROUPS).reshape(1, width)
    return pl.pallas_call(
        _sgu_sample_kernel,
        out_shape=[jax.ShapeDtypeStruct((M, D), F32), jax.ShapeDtypeStruct((M, width), F32)],
        compiler_params=pltpu.CompilerParams(vmem_limit_bytes=VMEM_LIMIT),
        name="sgu_sample",
    )(x, g.reshape(1, D), w_in, ln_g.reshape(1, width), ln_b.reshape(1, width), sw, sb, w_out)


def _sb_attn_kernel(q_ref, k_ref, v_ref, o_ref):
    i = pl.program_id(2)
    nq, nk = SB_QBLK, SB_KBLK
    lane = lax.broadcasted_iota(jnp.int32, (nq, LANES), 1)
    first_head = lane < HEAD_DIM
    later = _later(nk)
    q2 = q_ref[...] * ATTN_SCALE
    zero = jnp.zeros_like(q2)
    qs = jnp.concatenate([jnp.where(first_head, q2, zero), jnp.where(first_head, zero, q2)], axis=0)

    def rows_from(x, lo):
        return x if lo == 0 else jnp.concatenate([x[lo:nq], x[nq + lo:]], axis=0)

    def block(kb, o, c, lo):
        start = pl.multiple_of(kb * nk, nk)
        first = 0 if lo is None else lo
        n = nq - first
        z = _dot_nt(rows_from(qs, first), k_ref[pl.ds(start, nk), :])
        sp = _softplus(z)
        if lo is not None:
            r = lax.broadcasted_iota(jnp.int32, (2 * n, nk), 0)
            valid = lax.broadcasted_iota(jnp.int32, (2 * n, nk), 1) < jnp.where(r < n, r, r - n)
            sp = jnp.where(valid, sp, 0.0)
        cs = _dot(sp.astype(BF16), later)
        a = jnp.exp(z - sp - cs - rows_from(c, first))
        if lo is not None:
            a = jnp.where(valid, a, 0.0)
        do = _dot(a.astype(BF16), v_ref[pl.ds(start, nk), :])
        dc = cs[:, :1] + sp[:, :1]
        if first == 0:
            return o + do, c + dc
        pad = lambda d: jnp.concatenate(
            [jnp.zeros((first, d.shape[1]), F32), d[:n], jnp.zeros((first, d.shape[1]), F32), d[n:]], axis=0)
        return o + pad(do), c + pad(dc)

    o = jnp.zeros((2 * nq, LANES), F32)
    c = jnp.zeros((2 * nq, 1), F32)
    first_kb = i * (nq // nk)
    for d in reversed(range(nq // nk)):
        o, c = block(first_kb + d, o, c, d * nk)
    o, c = lax.fori_loop(0, first_kb, lambda s, oc: block(first_kb - 1 - s, oc[0], oc[1], None), (o, c))
    o_ref[...] = jnp.where(first_head, o[:nq], o[nq:]).astype(o_ref.dtype)


def sb_attn(qkv):
    B, S, _ = qkv.shape
    pairs = INNER // LANES
    return pl.pallas_call(
        _sb_attn_kernel,
        grid=(B, pairs, S // SB_QBLK),
        in_specs=[pl.BlockSpec((None, SB_QBLK, LANES), lambda b, p, i: (b, i, p)),
                  pl.BlockSpec((None, S, LANES), lambda b, p, i: (b, 0, pairs + p)),
                  pl.BlockSpec((None, S, LANES), lambda b, p, i: (b, 0, 2 * pairs + p))],
        out_specs=pl.BlockSpec((None, SB_QBLK, LANES), lambda b, p, i: (b, i, p)),
        out_shape=jax.ShapeDtypeStruct((B, S, INNER), BF16),
        compiler_params=_params("arbitrary", "arbitrary", "arbitrary"),
        name="sb_attn",
    )(qkv, qkv, qkv)


def _sb_sample_kernel(pt_ref, q_ref, *refs):
    del pt_ref
    page_refs = refs[:SB_PAGES_PER_STEP]
    o_ref, qcol_ref, acc_ref, c_ref = refs[SB_PAGES_PER_STEP:]
    p = pl.program_id(1)
    n = page_refs[0].shape[1]

    @pl.when(p == 0)
    def _():
        for k in range(SLABS):
            rows = slice(k * LANES, (k + 1) * LANES)
            qcol_ref[rows, :] = jnp.broadcast_to(_row_to_col(q_ref[:, rows]), (LANES, n))
        acc_ref[...] = jnp.zeros_like(acc_ref)
        c_ref[...] = jnp.zeros_like(c_ref)

    later = _later(n)
    for page_ref in page_refs:
        z = _head_sum(page_ref[:INNER, :] * qcol_ref[...], HEADS) * ATTN_SCALE
        sp = _softplus(z)
        a = jnp.exp(z - sp - _dot_split(sp, later) - c_ref[...])
        acc_ref[...] += page_ref[INNER:, :] * _head_rows(a, HEADS)
        c_ref[...] += jnp.sum(sp, axis=1, keepdims=True)

    @pl.when(p == pl.num_programs(1) - 1)
    def _():
        for k in range(SLABS):
            rows = slice(k * LANES, (k + 1) * LANES)
            o_ref[:, rows] = _col_to_row(jnp.sum(acc_ref[rows, :], axis=1, keepdims=True))


def sb_sample(q, pool, layer, page_table):
    DB = q.shape[0]
    layers, n_pool, page = pool.shape[:3]
    n_pages = page_table.shape[1]
    pps = SB_PAGES_PER_STEP
    assert n_pages % pps == 0
    pool_t = jnp.transpose(pool, (0, 1, 3, 4, 5, 2)).reshape(layers, n_pool, 2 * INNER, page)

    def page_spec(k):
        return pl.BlockSpec((None, None, 2 * INNER, page),
                            lambda b, p, pt: (layer, pt[b, n_pages - 1 - (p * pps + k)], 0, 0))

    o = pl.pallas_call(
        _sb_sample_kernel,
        grid_spec=pltpu.PrefetchScalarGridSpec(
            num_scalar_prefetch=1,
            grid=(DB, n_pages // pps),
            in_specs=[pl.BlockSpec((None, 1, INNER), lambda b, p, pt: (b, 0, 0))]
            + [page_spec(k) for k in range(pps)],
            out_specs=pl.BlockSpec((None, 1, INNER), lambda b, p, pt: (b, 0, 0)),
            scratch_shapes=[pltpu.VMEM((INNER, page), F32), pltpu.VMEM((INNER, page), F32),
                            pltpu.VMEM((HEADS, page), F32)]),
        out_shape=jax.ShapeDtypeStruct((DB, 1, INNER), F32),
        compiler_params=_params("arbitrary", "arbitrary"),
        name="sb_sample",
    )(page_table, q.reshape(DB, 1, INNER), *([pool_t] * pps))
    return o.reshape(DB, INNER)


def _ffn_ple_kernel(x_ref, p_ref, gf_ref, w1_ref, w2_ref, gp_ref, wproj_ref, wgate_ref, *rest, final):
    if final:
        gfin_ref, y_ref, h_ref, acc_ref = rest
    else:
        y_ref, h_ref, acc_ref = rest
    x = x_ref[...]
    h_ref[...] = _rms(x, gf_ref[...]).astype(BF16)
    hidden = w1_ref.shape[1]
    for c in range(hidden // FFN_CHUNK):
        cols = slice(c * FFN_CHUNK, (c + 1) * FFN_CHUNK)
        a = jnp.square(jnp.maximum(_dot(h_ref[...], w1_ref[:, cols]), 0.0))
        part = _dot(a.astype(BF16), w2_ref[cols, :])
        if c == 0:
            acc_ref[...] = part
        else:
            acc_ref[...] += part
    x2 = x + acc_ref[...]
    gate = jax.nn.sigmoid(_dot(_rms(x2, gp_ref[...]).astype(BF16), wgate_ref[...]))
    y = x2 + _dot(p_ref[...].astype(BF16), wproj_ref[...]) * gate
    if final:
        y = _rms(y, gfin_ref[...])
    y_ref[...] = y


def ffn_ple(x, layer, p, g_ffn, w1, w2, g_ple, w_proj, w_gate, g_final, tm):
    M, D = x.shape
    F = w1.shape[2]
    P = p.shape[2]
    const = lambda shape: pl.BlockSpec(shape, lambda i: (0, 0))
    in_specs = [pl.BlockSpec((tm, D), lambda i: (i, 0)),
                pl.BlockSpec((None, tm, P), lambda i: (layer, i, 0)),
                const((1, D)),
                pl.BlockSpec((None, D, F), lambda i: (layer, 0, 0)),
                pl.BlockSpec((None, F, D), lambda i: (layer, 0, 0)),
                const((1, D)),
                pl.BlockSpec((None, P, D), lambda i: (layer, 0, 0)),
                pl.BlockSpec((None, D, D), lambda i: (layer, 0, 0))]
    args = [x, p, g_ffn.reshape(1, D), w1, w2, g_ple.reshape(1, D), w_proj, w_gate]
    if g_final is not None:
        in_specs.append(const((1, D)))
        args.append(g_final.reshape(1, D))
    return pl.pallas_call(
        functools.partial(_ffn_ple_kernel, final=g_final is not None),
        grid=(M // tm,),
        in_specs=in_specs,
        out_specs=pl.BlockSpec((tm, D), lambda i: (i, 0)),
        out_shape=jax.ShapeDtypeStruct((M, D), F32),
        scratch_shapes=[pltpu.VMEM((tm, D), BF16), pltpu.VMEM((tm, D), F32)],
        compiler_params=_params("arbitrary"),
        name="ffn_ple",
    )(*args)


def kernel(x_prompt, x_sample, p_prompt, p_sample, cache_a_w128, cache_a_w512, cache_a_w2048, cache_c_kv, page_table, norm_mix, norm_ffn, norm_ple, norm_final, a_w_in, a_w_out, b_w_in, b_ln_g, b_ln_b, b_w_s, b_b_s, b_w_out, c_w_in, c_b_in, c_w_out, ffn_w1, ffn_w2, ple_w_proj, ple_w_gate):
    B, S, D = x_prompt.shape
    DB, T, _ = x_sample.shape
    assert T == 1, "the sample path handles one new token per sequence"
    assert S % A_TILE == 0 and S % SB_QBLK == 0
    depth = norm_mix.shape[0]
    MP, MS = B * S, DB * T
    tm = min(512, MP)
    bf = lambda w: w.astype(BF16)

    xp = x_prompt.reshape(MP, D)
    xs = x_sample.reshape(MS, D)
    caches_a = (cache_a_w128, cache_a_w512, cache_a_w2048)
    w1_all, w2_all, w_proj_all, w_gate_all = bf(ffn_w1), bf(ffn_w2), bf(ple_w_proj), bf(ple_w_gate)
    a_w_in_all = bf(a_w_in)
    a_x, a_gain = [], []
    a_s = [[] for _ in A_WINDOWS]
    b_s_rows, c_p, c_s = [], [], []

    for i in range(depth):
        kind, j = i % N_MIXERS, i // N_MIXERS
        if kind == 0:
            w_in, w_out = a_w_in_all[j], bf(a_w_out[j])
            qkv_p = group_matmul(norm_streams(xp, norm_mix[i]), a_w_in_all, j, A_TILE)
            outs, lses = zip(*[dil_attn(qkv_p, g, dil, S // A_TILE) for g, dil in enumerate(A_DILATIONS)])
            a_x.append(xp.reshape(B, S, D))
            a_gain.append(norm_mix[i])
            xp = merge_outproj(xp, outs, lses, w_out, tm)
            qkv_s = norm_matmul(xs, norm_mix[i], w_in, None, F32, MS, 1024)
            xs = outproj(xs, dil_sample(qkv_s, caches_a, j), w_out, MS)
            kv_s = qkv_s.reshape(DB, T, 3, A_GROUPS, HEADS, HEAD_DIM)
            for g in range(A_GROUPS):
                a_s[g].append(jnp.stack([kv_s[:, :, 1, g], kv_s[:, :, 2, g]], axis=2))
        elif kind == 1:
            w_in, w_out = bf(b_w_in[j]), bf(b_w_out[j])
            xp = sgu_prompt(xp, norm_mix[i], w_in, b_ln_g[j], b_ln_b[j], b_w_s[j], b_b_s[j], w_out, tm)
            xs, v_s = sgu_sample(xs, norm_mix[i], w_in, b_ln_g[j], b_ln_b[j], b_w_s[j], b_b_s[j], w_out)
            b_s_rows.append(v_s.reshape(DB, T, -1))
        else:
            w_in, w_out = bf(c_w_in[j]), bf(c_w_out[j])
            qkv_p = norm_matmul(xp, norm_mix[i], w_in, c_b_in[j], BF16, tm, 3 * INNER)
            kv_t = norm_matmul_t([xp.reshape(B, S, D)], norm_mix[i][None], jnp.transpose(w_in[:, INNER:])[None],
                                 c_b_in[j][None, INNER:], S)
            c_p.append(kv_rows_from_t(kv_t)[0])
            qkv_s = norm_matmul(xs, norm_mix[i], w_in, c_b_in[j], F32, MS, 1024)
            o_p = sb_attn(qkv_p.reshape(B, S, -1))
            xp = outproj(xp, o_p.reshape(MP, INNER), w_out, tm)
            xs = outproj(xs, sb_sample(qkv_s[:, :INNER], cache_c_kv, j, page_table), w_out, MS)
            c_s.append(qkv_s[:, INNER:].reshape(DB, T, 2, HEADS, HEAD_DIM))
        g_final = norm_final if i == depth - 1 else None
        tail = (norm_ffn[i], w1_all, w2_all, norm_ple[i], w_proj_all, w_gate_all, g_final)
        xp = ffn_ple(xp, i, p_prompt.reshape(depth, MP, -1), *tail, tm)
        xs = ffn_ple(xs, i, p_sample.reshape(depth, MS, -1), *tail, MS)

    w5 = a_w_in_all.reshape(-1, D, 3, A_GROUPS, INNER)
    a_p = []
    for g, win in enumerate(A_WINDOWS):
        wt_kv = jnp.transpose(w5[:, :, 1:, g, :].reshape(-1, D, 2 * INNER), (0, 2, 1))
        a_p.append(kv_rows_from_t(norm_matmul_t(a_x, jnp.stack(a_gain), wt_kv, None, min(win, S))))

    return (xp.reshape(B, S, D), xs.reshape(DB, T, D),
            a_p[0], a_p[1], a_p[2],
            jnp.stack(a_s[0]), jnp.stack(a_s[1]), jnp.stack(a_s[2]),
            jnp.stack(b_s_rows), jnp.stack(c_p), jnp.stack(c_s))
```

```python
import functools
import math

import jax
import jax.numpy as jnp
from jax import lax
from jax.experimental import pallas as pl
from jax.experimental.pallas import tpu as pltpu

F32 = jnp.float32
BF16 = jnp.bfloat16

HEAD_DIM = 64
HEADS = 16
INNER = HEADS * HEAD_DIM
ATTN_SCALE = 1.0 / math.sqrt(HEAD_DIM)
LOG2E = 1.4426950408889634
BLK = 128
A_WINDOWS = (128, 512, 2048)
A_DILATIONS = (1, 4, 16)
A_GROUPS = 3
A_TILE = 2048
A_UNITS = A_TILE // BLK
B_GROUPS = 8
CHUNK = 128
SB_QBLK = 1024
SB_KBLK = 256
SB_BLOCKS_PER_ITER = 4
SB_PAGES_PER_STEP = 8
FFN_CHUNK = 512
SAMPLE_HEAD_GROUPS = 4
N_MIXERS = 3
EPS = 1e-6
LANES = 128
SLABS = INNER // LANES
VMEM_LIMIT = 56 * 1024 * 1024


def _params(*sem):
    return pltpu.CompilerParams(dimension_semantics=sem, vmem_limit_bytes=VMEM_LIMIT)


def _rms(xf, g):
    return xf * lax.rsqrt(jnp.mean(xf * xf, axis=-1, keepdims=True) + EPS) * g


def _dot(a, b):
    return jnp.dot(a, b, preferred_element_type=F32)


def _dot_nt(a, b):
    return lax.dot_general(a, b, (((1,), (1,)), ((), ())), preferred_element_type=F32)


def _dot_split(a, b_bf16):
    rows = a.shape[0]
    hi = a.astype(BF16)
    lo = (a - hi.astype(F32)).astype(BF16)
    both = _dot(jnp.concatenate([hi, lo], axis=0), b_bf16)
    return both[:rows] + both[rows:]


def _softplus(z):
    return jnp.maximum(z, 0.0) + jnp.log(1.0 + jnp.exp2(jnp.abs(z) * (-LOG2E)))


def _later(n):
    return (lax.broadcasted_iota(jnp.int32, (n, n), 0) > lax.broadcasted_iota(jnp.int32, (n, n), 1)).astype(BF16)


def _eye(n):
    return lax.broadcasted_iota(jnp.int32, (n, n), 0) == lax.broadcasted_iota(jnp.int32, (n, n), 1)


def _row_to_col(row):
    n = row.shape[1]
    return jnp.sum(jnp.where(_eye(n), row, 0.0), axis=1, keepdims=True)


def _col_to_row(col):
    n = col.shape[0]
    return jnp.sum(jnp.where(_eye(n), col, 0.0), axis=0, keepdims=True)


def _head_sum(x, heads):
    return jnp.sum(x.reshape(heads, HEAD_DIM, x.shape[1]), axis=1)


def _head_rows(x, heads):
    return jnp.broadcast_to(x[:, None, :], (heads, HEAD_DIM, x.shape[1])).reshape(heads * HEAD_DIM, x.shape[1])


def _norm_matmul_kernel(x_ref, g_ref, w_ref, *rest, has_bias):
    if has_bias:
        b_ref, o_ref, h_ref = rest
    else:
        o_ref, h_ref = rest

    @pl.when(pl.program_id(1) == 0)
    def _():
        h_ref[...] = _rms(x_ref[...], g_ref[...]).astype(BF16)

    acc = _dot(h_ref[...], w_ref[...])
    if has_bias:
        acc = acc + b_ref[...]
    o_ref[...] = acc.astype(o_ref.dtype)


def norm_matmul(x, g, w, bias, out_dtype, tm, tn):
    M, D = x.shape
    N = w.shape[1]
    in_specs = [pl.BlockSpec((tm, D), lambda i, j: (i, 0)),
                pl.BlockSpec((1, D), lambda i, j: (0, 0)),
                pl.BlockSpec((D, tn), lambda i, j: (0, j))]
    args = [x, g.reshape(1, D), w]
    if bias is not None:
        in_specs.append(pl.BlockSpec((1, tn), lambda i, j: (0, j)))
        args.append(bias.reshape(1, N))
    return pl.pallas_call(
        functools.partial(_norm_matmul_kernel, has_bias=bias is not None),
        grid=(M // tm, N // tn),
        in_specs=in_specs,
        out_specs=pl.BlockSpec((tm, tn), lambda i, j: (i, j)),
        out_shape=jax.ShapeDtypeStruct((M, N), out_dtype),
        scratch_shapes=[pltpu.VMEM((tm, D), BF16)],
        compiler_params=_params("arbitrary", "arbitrary"),
        name="norm_matmul",
    )(*args)


def _norm_matmul_t_kernel(*refs, n_layers, has_bias):
    x_refs = refs[:n_layers]
    g_ref, wt_ref = refs[n_layers:n_layers + 2]
    if has_bias:
        b_ref, o_ref = refs[n_layers + 2:]
    else:
        (o_ref,) = refs[n_layers + 2:]
    for layer, x_ref in enumerate(x_refs):
        @pl.when(pl.program_id(0) == layer)
        def _(x_ref=x_ref):
            h = _rms(x_ref[...], g_ref[...]).astype(BF16)
            acc = _dot_nt(wt_ref[...], h)
            if has_bias:
                acc = acc + b_ref[...]
            o_ref[...] = acc


def norm_matmul_t(xs, g, wt, bias, keep):
    n_layers = len(xs)
    B, S, D = xs[0].shape
    N = wt.shape[1]
    tm = min(512, keep)
    first = (S - keep) // tm
    in_specs = [pl.BlockSpec((None, tm, D), lambda l, b, i: (b, first + i, 0))] * n_layers
    in_specs += [pl.BlockSpec((None, 1, D), lambda l, b, i: (l, 0, 0)),
                 pl.BlockSpec((None, N, D), lambda l, b, i: (l, 0, 0))]
    args = list(xs) + [g.reshape(n_layers, 1, D), wt]
    if bias is not None:
        in_specs.append(pl.BlockSpec((None, N, 1), lambda l, b, i: (l, 0, 0)))
        args.append(bias.reshape(n_layers, N, 1))
    return pl.pallas_call(
        functools.partial(_norm_matmul_t_kernel, n_layers=n_layers, has_bias=bias is not None),
        grid=(n_layers, B, keep // tm),
        in_specs=in_specs,
        out_specs=pl.BlockSpec((None, None, N, tm), lambda l, b, i: (l, b, 0, i)),
        out_shape=jax.ShapeDtypeStruct((n_layers, B, N, keep), F32),
        compiler_params=_params("arbitrary", "arbitrary", "arbitrary"),
        name="norm_matmul_t",
    )(*args)


def kv_rows_from_t(kv_t):
    L, B, _, keep = kv_t.shape
    return jnp.transpose(kv_t.reshape(L, B, 2, HEADS, HEAD_DIM, keep), (0, 1, 5, 2, 3, 4))


def _outproj_kernel(x_ref, o_ref, w_ref, y_ref):
    y_ref[...] = x_ref[...] + _dot(o_ref[...].astype(BF16), w_ref[...])


def outproj(x, o, w, tm):
    M, D = x.shape
    K = o.shape[1]
    return pl.pallas_call(
        _outproj_kernel,
        grid=(M // tm,),
        in_specs=[pl.BlockSpec((tm, D), lambda i: (i, 0)),
                  pl.BlockSpec((tm, K), lambda i: (i, 0)),
                  pl.BlockSpec((K, D), lambda i: (0, 0))],
        out_specs=pl.BlockSpec((tm, D), lambda i: (i, 0)),
        out_shape=jax.ShapeDtypeStruct((M, D), F32),
        compiler_params=_params("arbitrary"),
        name="outproj",
    )(x, o, w)


def _merge_outproj_kernel(x_ref, o0_ref, o1_ref, o2_ref, l0_ref, l1_ref, l2_ref, w_ref, y_ref):
    l0, l1, l2 = l0_ref[...], l1_ref[...], l2_ref[...]
    m = jnp.maximum(jnp.maximum(l0, l1), l2)
    e0, e1, e2 = jnp.exp(l0 - m), jnp.exp(l1 - m), jnp.exp(l2 - m)
    den = e0 + e1 + e2
    expand = (lax.broadcasted_iota(jnp.int32, (LANES, INNER), 1) // HEAD_DIM
              == lax.broadcasted_iota(jnp.int32, (LANES, INNER), 0)).astype(BF16)
    inv = 1.0 / den
    o = (_dot((e0 * inv).astype(BF16), expand) * o0_ref[...].astype(F32)
         + _dot((e1 * inv).astype(BF16), expand) * o1_ref[...].astype(F32)
         + _dot((e2 * inv).astype(BF16), expand) * o2_ref[...].astype(F32))
    y_ref[...] = x_ref[...] + _dot(o.astype(BF16), w_ref[...])


def merge_outproj(x, outs, lses, w, tm):
    M, D = x.shape
    row = lambda width: pl.BlockSpec((tm, width), lambda i: (i, 0))
    return pl.pallas_call(
        _merge_outproj_kernel,
        grid=(M // tm,),
        in_specs=[row(D)] + [row(INNER)] * A_GROUPS + [row(LANES)] * A_GROUPS
        + [pl.BlockSpec((INNER, D), lambda i: (0, 0))],
        out_specs=row(D),
        out_shape=jax.ShapeDtypeStruct((M, D), F32),
        compiler_params=_params("arbitrary"),
        name="merge_outproj",
    )(x, *outs, *lses, w)


def _norm_streams_kernel(x_ref, g_ref, h_ref, slab_ref):
    grp = pl.program_id(1)

    @pl.when(grp == 0)
    def _():
        rows = 256
        for c in range(A_TILE // rows):
            sl = slice(c * rows, (c + 1) * rows)
            h = _rms(x_ref[sl, :], g_ref[...])
            h_ref[sl, :] = h.astype(BF16)
            for k in range(SLABS):
                slab_ref[k, sl, :] = h[:, k * LANES:(k + 1) * LANES]

    for g, dil in enumerate(A_DILATIONS):
        if dil == 1:
            continue
        per_stream = A_TILE // dil

        @pl.when(grp == g)
        def _(dil=dil, per_stream=per_stream):
            for r in range(dil):
                for k in range(SLABS):
                    h_ref[r * per_stream:(r + 1) * per_stream, k * LANES:(k + 1) * LANES] = (
                        slab_ref[k, pl.ds(r, per_stream, stride=dil), :].astype(BF16))


def norm_streams(x, g):
    M, D = x.shape
    assert A_DILATIONS[0] == 1
    return pl.pallas_call(
        _norm_streams_kernel,
        grid=(M // A_TILE, A_GROUPS),
        in_specs=[pl.BlockSpec((A_TILE, D), lambda i, g: (i, 0)), pl.BlockSpec((1, D), lambda i, g: (0, 0))],
        out_specs=pl.BlockSpec((None, A_TILE, D), lambda i, g: (g, i, 0)),
        out_shape=jax.ShapeDtypeStruct((A_GROUPS, M, D), BF16),
        scratch_shapes=[pltpu.VMEM((SLABS, A_TILE, LANES), F32)],
        compiler_params=_params("arbitrary", "arbitrary"),
        name="norm_streams",
    )(x, g.reshape(1, D))


def _group_matmul_kernel(h_ref, w_ref, o_ref):
    o_ref[...] = _dot(h_ref[...], w_ref[...]).astype(o_ref.dtype)


def group_matmul(h, w, layer, tm):
    G, M, D = h.shape
    P = w.shape[2] // (G * INNER)
    return pl.pallas_call(
        _group_matmul_kernel,
        grid=(G, M // tm, P),
        in_specs=[pl.BlockSpec((None, tm, D), lambda g, i, j: (g, i, 0)),
                  pl.BlockSpec((None, D, INNER), lambda g, i, j: (layer, 0, j * G + g))],
        out_specs=pl.BlockSpec((None, None, tm, INNER), lambda g, i, j: (g, j, i, 0)),
        out_shape=jax.ShapeDtypeStruct((G, P, M, INNER), BF16),
        compiler_params=_params("arbitrary", "arbitrary", "arbitrary"),
        name="group_matmul",
    )(h, w)


def _dil_attn_kernel(q_ref, kp_ref, kc_ref, vp_ref, vc_ref, o_ref, lse_ref, o_slab, lse_slab, *,
                     dil, tiles_per_seq):
    per_stream = A_UNITS // dil
    u = pl.program_id(1)
    r, t = u // per_stream, u % per_stream
    has_prev = (t > 0) | (pl.program_id(0) % tiles_per_seq > 0)
    row = lax.broadcasted_iota(jnp.int32, (BLK, 2 * BLK), 0)
    col = lax.broadcasted_iota(jnp.int32, (BLK, 2 * BLK), 1) & (BLK - 1)
    mask_c = col <= row
    mask_p = (col >= row) & has_prev
    lane = lax.broadcasted_iota(jnp.int32, (BLK, LANES), 1)
    first_head = lane < HEAD_DIM
    rows2 = lax.broadcasted_iota(jnp.int32, (2 * BLK, LANES), 0)
    lanes2 = lax.broadcasted_iota(jnp.int32, (2 * BLK, LANES), 1)
    ones_sel = ((rows2 < BLK) == (lanes2 < HEAD_DIM)).astype(BF16)
    lse_tile = jnp.zeros((BLK, LANES), F32)
    start = r + dil * BLK * t

    def two_heads(x):
        zero = jnp.zeros_like(x)
        return jnp.concatenate([jnp.where(first_head, x, zero), jnp.where(first_head, zero, x)], axis=0)

    for pair in range(HEADS // 2):
        sl = slice(pair * LANES, (pair + 1) * LANES)
        q2 = q_ref[:, sl] * ATTN_SCALE
        s_c = jnp.where(mask_c, _dot_nt(q2, two_heads(kc_ref[:, sl])), -jnp.inf)
        s_p = jnp.where(mask_p, _dot_nt(q2, two_heads(kp_ref[:, sl])), -jnp.inf)
        s_max = jnp.maximum(s_c, s_p)
        m0 = jnp.max(s_max[:, :BLK], axis=1, keepdims=True)
        m1 = jnp.max(s_max[:, BLK:], axis=1, keepdims=True)
        m = jnp.concatenate([jnp.broadcast_to(m0, (BLK, BLK)), jnp.broadcast_to(m1, (BLK, BLK))], axis=1)
        e_c = jnp.exp(s_c - m).astype(BF16)
        e_p = jnp.exp(s_p - m).astype(BF16)
        v_c = jnp.concatenate([two_heads(vc_ref[:, sl]), ones_sel], axis=1)
        v_p = jnp.concatenate([two_heads(vp_ref[:, sl]), ones_sel], axis=1)
        res = _dot(e_c, v_c) + _dot(e_p, v_p)
        den = res[:, LANES:]
        out = res[:, :LANES] / den
        if dil == 1:
            o_slab[pair, pl.ds(start, BLK), :] = out
        else:
            o_slab[pair, pl.ds(start, BLK, stride=dil), :] = out
        lse_tile = jnp.where(lane == 2 * pair, m0 + jnp.log(den[:, :1]), lse_tile)
        lse_tile = jnp.where(lane == 2 * pair + 1, m1 + jnp.log(den[:, HEAD_DIM:HEAD_DIM + 1]), lse_tile)
    if dil == 1:
        lse_slab[pl.ds(start, BLK), :] = lse_tile
    else:
        lse_slab[pl.ds(start, BLK, stride=dil), :] = lse_tile

    @pl.when(u == A_UNITS - 1)
    def _():
        for k in range(SLABS):
            o_ref[:, k * LANES:(k + 1) * LANES] = o_slab[k].astype(o_ref.dtype)
        lse_ref[...] = lse_slab[...]


def dil_attn(qkv, g, dil, tiles_per_seq):
    M = qkv.shape[2]
    per_stream = A_UNITS // dil

    def cur(part):
        return pl.BlockSpec((None, None, BLK, INNER), lambda c, u: (g, part, c * A_UNITS + u, 0))

    def prev(part):
        def index(c, u):
            blk = c * A_UNITS + u
            t = u % per_stream
            in_tile = blk - 1
            in_prev_tile = blk - A_UNITS + per_stream - 1
            has_prev_tile = c % tiles_per_seq > 0
            return (g, part, jnp.where(t > 0, in_tile, jnp.where(has_prev_tile, in_prev_tile, blk)), 0)
        return pl.BlockSpec((None, None, BLK, INNER), index)

    return pl.pallas_call(
        functools.partial(_dil_attn_kernel, dil=dil, tiles_per_seq=tiles_per_seq),
        grid=(M // A_TILE, A_UNITS),
        in_specs=[cur(0), prev(1), cur(1), prev(2), cur(2)],
        out_specs=[pl.BlockSpec((A_TILE, INNER), lambda c, u: (c, 0)),
                   pl.BlockSpec((A_TILE, LANES), lambda c, u: (c, 0))],
        out_shape=[jax.ShapeDtypeStruct((M, INNER), BF16), jax.ShapeDtypeStruct((M, LANES), F32)],
        scratch_shapes=[pltpu.VMEM((SLABS, A_TILE, LANES), F32), pltpu.VMEM((A_TILE, LANES), F32)],
        compiler_params=_params("arbitrary", "arbitrary"),
        name=f"dil_attn_d{dil}",
    )(qkv, qkv, qkv, qkv, qkv)


def _dil_sample_kernel(*refs):
    new_refs = refs[:3 * A_GROUPS]
    cache_refs = refs[3 * A_GROUPS:5 * A_GROUPS]
    o_ref = refs[5 * A_GROUPS]
    heads = HEADS // SAMPLE_HEAD_GROUPS
    scores, new_scores = [], []
    for g, dil in enumerate(A_DILATIONS):
        q_col = _row_to_col(new_refs[g][...])
        k_col = _row_to_col(new_refs[A_GROUPS + g][...])
        kt = cache_refs[2 * g][...]
        s = _head_sum(kt * q_col, heads) * ATTN_SCALE
        pos = lax.broadcasted_iota(jnp.int32, s.shape, 1)
        scores.append(jnp.where((pos & (dil - 1)) == 0, s, -jnp.inf))
        new_scores.append(_head_sum(q_col * k_col, heads) * ATTN_SCALE)
    m = new_scores[0]
    for g in range(A_GROUPS):
        m = jnp.maximum(m, jnp.maximum(new_scores[g], jnp.max(scores[g], axis=1, keepdims=True)))
    num = jnp.zeros((heads * HEAD_DIM, 1), F32)
    den = jnp.zeros((heads, 1), F32)
    for g in range(A_GROUPS):
        v_col = _row_to_col(new_refs[2 * A_GROUPS + g][...])
        e = jnp.exp(scores[g] - m)
        e_new = jnp.exp(new_scores[g] - m)
        vt = cache_refs[2 * g + 1][...]
        num = num + jnp.sum(vt * _head_rows(e, heads), axis=1, keepdims=True) + _head_rows(e_new, heads) * v_col
        den = den + jnp.sum(e, axis=1, keepdims=True) + e_new
    o_ref[...] = _col_to_row(num / _head_rows(den, heads))


def dil_sample(qkv, caches, layer):
    DB, W = qkv.shape
    hgs = SAMPLE_HEAD_GROUPS
    width = INNER // hgs
    new_specs = [pl.BlockSpec((None, 1, width), lambda b, h, c=(part * A_GROUPS + g) * hgs: (b, 0, c + h))
                 for part in range(3) for g in range(A_GROUPS)]
    views, specs = [], []
    for cache, win, dil in zip(caches, A_WINDOWS, A_DILATIONS):
        assert cache.shape[2] == win and win // dil == BLK and dil & (dil - 1) == 0
        view = jnp.transpose(cache, (0, 1, 3, 4, 5, 2)).reshape(cache.shape[0], DB, 2 * INNER, win)
        views += [view, view]
        specs += [pl.BlockSpec((None, None, width, win), lambda b, h: (layer, b, h, 0)),
                  pl.BlockSpec((None, None, width, win), lambda b, h: (layer, b, hgs + h, 0))]
    qkv3 = qkv.reshape(DB, 1, W)
    o = pl.pallas_call(
        _dil_sample_kernel,
        grid=(DB, hgs),
        in_specs=new_specs + specs,
        out_specs=pl.BlockSpec((None, 1, width), lambda b, h: (b, 0, h)),
        out_shape=jax.ShapeDtypeStruct((DB, 1, INNER), F32),
        compiler_params=_params("arbitrary", "arbitrary"),
        name="dil_sample",
    )(*([qkv3] * (3 * A_GROUPS)), *views)
    return o.reshape(DB, INNER)


def _sgu_front(x, g_ref, win_ref, lng_ref, lnb_ref):
    z = jax.nn.gelu(_dot(_rms(x, g_ref[...]).astype(BF16), win_ref[...]))
    width = z.shape[1] // 2
    u, vr = z[:, :width], z[:, width:]
    mu = jnp.mean(vr, axis=-1, keepdims=True)
    var = jnp.mean(jnp.square(vr - mu), axis=-1, keepdims=True)
    return u, (vr - mu) * lax.rsqrt(var + EPS) * lng_ref[...] + lnb_ref[...]


def _sgu_prompt_kernel(x_ref, g_ref, win_ref, lng_ref, lnb_ref, ws_ref, bs_ref, wout_ref, y_ref, us_ref):
    x = x_ref[...]
    u, v = _sgu_front(x, g_ref, win_ref, lng_ref, lnb_ref)
    causal = (lax.broadcasted_iota(jnp.int32, (CHUNK, CHUNK), 0)
              >= lax.broadcasted_iota(jnp.int32, (CHUNK, CHUNK), 1))
    for grp in range(B_GROUPS):
        cols = slice(grp * CHUNK, (grp + 1) * CHUNK)
        wm = jnp.where(causal, ws_ref[grp], 0.0).astype(BF16)
        for c in range(x.shape[0] // CHUNK):
            rows = slice(c * CHUNK, (c + 1) * CHUNK)
            s = _dot(wm, v[rows, cols].astype(BF16)) + bs_ref[:, cols]
            us_ref[rows, cols] = (u[rows, cols] * s).astype(BF16)
    y_ref[...] = x + _dot(us_ref[...], wout_ref[...])


def sgu_prompt(x, g, w_in, ln_g, ln_b, w_s, b_s, w_out, tm):
    M, D = x.shape
    width = w_out.shape[0]
    bias = jnp.repeat(b_s.T, width // B_GROUPS, axis=1)
    const = lambda shape: pl.BlockSpec(shape, lambda i: (0,) * len(shape))
    return pl.pallas_call(
        _sgu_prompt_kernel,
        grid=(M // tm,),
        in_specs=[pl.BlockSpec((tm, D), lambda i: (i, 0)), const((1, D)), const((D, 2 * width)),
                  const((1, width)), const((1, width)), const((B_GROUPS, CHUNK, CHUNK)),
                  const((CHUNK, width)), const((width, D))],
        out_specs=pl.BlockSpec((tm, D), lambda i: (i, 0)),
        out_shape=jax.ShapeDtypeStruct((M, D), F32),
        scratch_shapes=[pltpu.VMEM((tm, width), BF16)],
        compiler_params=_params("arbitrary"),
        name="sgu_prompt",
    )(x, g.reshape(1, D), w_in, ln_g.reshape(1, width), ln_b.reshape(1, width), w_s, bias, w_out)


def _sgu_sample_kernel(x_ref, g_ref, win_ref, lng_ref, lnb_ref, sw_ref, sb_ref, wout_ref, y_ref, v_ref):
    x = x_ref[...]
    u, v = _sgu_front(x, g_ref, win_ref, lng_ref, lnb_ref)
    s = sw_ref[...] * v + sb_ref[...]
    y_ref[...] = x + _dot((u * s).astype(BF16), wout_ref[...])
    v_ref[...] = v


def sgu_sample(x, g, w_in, ln_g, ln_b, w_s, b_s, w_out):
    M, D = x.shape
    width = w_out.shape[0]
    sw = jnp.repeat(w_s[:, 0, 0], width // B_GROUPS).reshape(1, width)
    sb = jnp.repeat(b_s[:, 0], width // B_GROUPS).reshape(1, width)
    return pl.pallas_call(
        _sgu_sample_kernel,
        out_shape=[jax.ShapeDtypeStruct((M, D), F32), jax.ShapeDtypeStruct((M, width), F32)],
        compiler_params=pltpu.CompilerParams(vmem_limit_bytes=VMEM_LIMIT),
        name="sgu_sample",
    )(x, g.reshape(1, D), w_in, ln_g.reshape(1, width), ln_b.reshape(1, width), sw, sb, w_out)


def _sb_attn_kernel(q_ref, k_ref, v_ref, o_ref):
    i = pl.program_id(2)
    nq, nk = SB_QBLK, SB_KBLK
    lane = lax.broadcasted_iota(jnp.int32, (nq, LANES), 1)
    first_head = lane < HEAD_DIM
    later = _later(nk)
    q2 = q_ref[...] * ATTN_SCALE
    zero = jnp.zeros_like(q2)
    qs = jnp.concatenate([jnp.where(first_head, q2, zero), jnp.where(first_head, zero, q2)], axis=0)

    def rows_from(x, lo):
        return x if lo == 0 else jnp.concatenate([x[lo:nq], x[nq + lo:]], axis=0)

    def weights(kb, c, lo):
        start = pl.multiple_of(kb * nk, nk)
        first = 0 if lo is None else lo
        n = nq - first
        z = _dot_nt(rows_from(qs, first), k_ref[pl.ds(start, nk), :])
        sp = _softplus(z)
        if lo is not None:
            r = lax.broadcasted_iota(jnp.int32, (2 * n, nk), 0)
            valid = lax.broadcasted_iota(jnp.int32, (2 * n, nk), 1) < jnp.where(r < n, r, r - n)
            sp = jnp.where(valid, sp, 0.0)
        cs = _dot(sp.astype(BF16), later)
        a = jnp.exp(z - sp - cs - rows_from(c, first))
        if lo is not None:
            a = jnp.where(valid, a, 0.0)
        return a.astype(BF16), cs[:, :1] + sp[:, :1], start

    def diagonal_block(kb, o, c, lo):
        a, dc, start = weights(kb, c, lo)
        do = _dot(a, v_ref[pl.ds(start, nk), :])
        if lo == 0:
            return o + do, c + dc
        n = nq - lo
        pad = lambda d: jnp.concatenate(
            [jnp.zeros((lo, d.shape[1]), F32), d[:n], jnp.zeros((lo, d.shape[1]), F32), d[n:]], axis=0)
        return o + pad(do), c + pad(dc)

    def earlier_blocks(s, oc):
        o, c = oc
        kb = first_kb - 1 - SB_BLOCKS_PER_ITER * s
        do = None
        for j in range(SB_BLOCKS_PER_ITER):
            a, dc, start = weights(kb - j, c, None)
            c = c + dc
            part = _dot(a, v_ref[pl.ds(start, nk), :])
            do = part if do is None else do + part
        return o + do, c

    o = jnp.zeros((2 * nq, LANES), F32)
    c = jnp.zeros((2 * nq, 1), F32)
    first_kb = i * (nq // nk)
    for d in reversed(range(nq // nk)):
        o, c = diagonal_block(first_kb + d, o, c, d * nk)
    o, c = lax.fori_loop(0, first_kb // SB_BLOCKS_PER_ITER, earlier_blocks, (o, c))
    o_ref[...] = jnp.where(first_head, o[:nq], o[nq:]).astype(o_ref.dtype)


def sb_attn(qkv):
    B, S, _ = qkv.shape
    pairs = INNER // LANES
    return pl.pallas_call(
        _sb_attn_kernel,
        grid=(B, pairs, S // SB_QBLK),
        in_specs=[pl.BlockSpec((None, SB_QBLK, LANES), lambda b, p, i: (b, i, p)),
                  pl.BlockSpec((None, S, LANES), lambda b, p, i: (b, 0, pairs + p)),
                  pl.BlockSpec((None, S, LANES), lambda b, p, i: (b, 0, 2 * pairs + p))],
        out_specs=pl.BlockSpec((None, SB_QBLK, LANES), lambda b, p, i: (b, i, p)),
        out_shape=jax.ShapeDtypeStruct((B, S, INNER), BF16),
        compiler_params=_params("arbitrary", "arbitrary", "arbitrary"),
        name="sb_attn",
    )(qkv, qkv, qkv)


def _sb_sample_kernel(pt_ref, q_ref, *refs):
    del pt_ref
    page_refs = refs[:SB_PAGES_PER_STEP]
    o_ref, qcol_ref, acc_ref, c_ref = refs[SB_PAGES_PER_STEP:]
    p = pl.program_id(1)
    n = page_refs[0].shape[1]

    @pl.when(p == 0)
    def _():
        for k in range(SLABS):
            rows = slice(k * LANES, (k + 1) * LANES)
            qcol_ref[rows, :] = jnp.broadcast_to(_row_to_col(q_ref[:, rows]), (LANES, n))
        acc_ref[...] = jnp.zeros_like(acc_ref)
        c_ref[...] = jnp.zeros_like(c_ref)

    later = _later(n)
    for page_ref in page_refs:
        z = _head_sum(page_ref[:INNER, :] * qcol_ref[...], HEADS) * ATTN_SCALE
        sp = _softplus(z)
        a = jnp.exp(z - sp - _dot_split(sp, later) - c_ref[...])
        acc_ref[...] += page_ref[INNER:, :] * _head_rows(a, HEADS)
        c_ref[...] += jnp.sum(sp, axis=1, keepdims=True)

    @pl.when(p == pl.num_programs(1) - 1)
    def _():
        for k in range(SLABS):
            rows = slice(k * LANES, (k + 1) * LANES)
            o_ref[:, rows] = _col_to_row(jnp.sum(acc_ref[rows, :], axis=1, keepdims=True))


def sb_sample(q, pool, layer, page_table):
    DB = q.shape[0]
    layers, n_pool, page = pool.shape[:3]
    n_pages = page_table.shape[1]
    pps = SB_PAGES_PER_STEP
    assert n_pages % pps == 0
    pool_t = jnp.transpose(pool, (0, 1, 3, 4, 5, 2)).reshape(layers, n_pool, 2 * INNER, page)

    def page_spec(k):
        return pl.BlockSpec((None, None, 2 * INNER, page),
                            lambda b, p, pt: (layer, pt[b, n_pages - 1 - (p * pps + k)], 0, 0))

    o = pl.pallas_call(
        _sb_sample_kernel,
        grid_spec=pltpu.PrefetchScalarGridSpec(
            num_scalar_prefetch=1,
            grid=(DB, n_pages // pps),
            in_specs=[pl.BlockSpec((None, 1, INNER), lambda b, p, pt: (b, 0, 0))]
            + [page_spec(k) for k in range(pps)],
            out_specs=pl.BlockSpec((None, 1, INNER), lambda b, p, pt: (b, 0, 0)),
            scratch_shapes=[pltpu.VMEM((INNER, page), F32), pltpu.VMEM((INNER, page), F32),
                            pltpu.VMEM((HEADS, page), F32)]),
        out_shape=jax.ShapeDtypeStruct((DB, 1, INNER), F32),
        compiler_params=_params("arbitrary", "arbitrary"),
        name="sb_sample",
    )(page_table, q.reshape(DB, 1, INNER), *([pool_t] * pps))
    return o.reshape(DB, INNER)


def _ffn_ple_kernel(x_ref, p_ref, gf_ref, w1_ref, w2_ref, gp_ref, wproj_ref, wgate_ref, *rest, final):
    if final:
        gfin_ref, y_ref, h_ref, acc_ref = rest
    else:
        y_ref, h_ref, acc_ref = rest
    x = x_ref[...]
    h_ref[...] = _rms(x, gf_ref[...]).astype(BF16)
    hidden = w1_ref.shape[1]
    for c in range(hidden // FFN_CHUNK):
        cols = slice(c * FFN_CHUNK, (c + 1) * FFN_CHUNK)
        a = jnp.square(jnp.maximum(_dot(h_ref[...], w1_ref[:, cols]), 0.0))
        part = _dot(a.astype(BF16), w2_ref[cols, :])
        if c == 0:
            acc_ref[...] = part
        else:
            acc_ref[...] += part
    x2 = x + acc_ref[...]
    gate = jax.nn.sigmoid(_dot(_rms(x2, gp_ref[...]).astype(BF16), wgate_ref[...]))
    y = x2 + _dot(p_ref[...].astype(BF16), wproj_ref[...]) * gate
    if final:
        y = _rms(y, gfin_ref[...])
    y_ref[...] = y


def ffn_ple(x, layer, p, g_ffn, w1, w2, g_ple, w_proj, w_gate, g_final, tm):
    M, D = x.shape
    F = w1.shape[2]
    P = p.shape[2]
    const = lambda shape: pl.BlockSpec(shape, lambda i: (0, 0))
    in_specs = [pl.BlockSpec((tm, D), lambda i: (i, 0)),
                pl.BlockSpec((None, tm, P), lambda i: (layer, i, 0)),
                const((1, D)),
                pl.BlockSpec((None, D, F), lambda i: (layer, 0, 0)),
                pl.BlockSpec((None, F, D), lambda i: (layer, 0, 0)),
                const((1, D)),
                pl.BlockSpec((None, P, D), lambda i: (layer, 0, 0)),
                pl.BlockSpec((None, D, D), lambda i: (layer, 0, 0))]
    args = [x, p, g_ffn.reshape(1, D), w1, w2, g_ple.reshape(1, D), w_proj, w_gate]
    if g_final is not None:
        in_specs.append(const((1, D)))
        args.append(g_final.reshape(1, D))
    return pl.pallas_call(
        functools.partial(_ffn_ple_kernel, final=g_final is not None),
        grid=(M // tm,),
        in_specs=in_specs,
        out_specs=pl.BlockSpec((tm, D), lambda i: (i, 0)),
        out_shape=jax.ShapeDtypeStruct((M, D), F32),
        scratch_shapes=[pltpu.VMEM((tm, D), BF16), pltpu.VMEM((tm, D), F32)],
        compiler_params=_params("arbitrary"),
        name="ffn_ple",
    )(*args)


def kernel(x_prompt, x_sample, p_prompt, p_sample, cache_a_w128, cache_a_w512, cache_a_w2048, cache_c_kv, page_table, norm_mix, norm_ffn, norm_ple, norm_final, a_w_in, a_w_out, b_w_in, b_ln_g, b_ln_b, b_w_s, b_b_s, b_w_out, c_w_in, c_b_in, c_w_out, ffn_w1, ffn_w2, ple_w_proj, ple_w_gate):
    B, S, D = x_prompt.shape
    DB, T, _ = x_sample.shape
    assert T == 1, "the sample path handles one new token per sequence"
    assert S % A_TILE == 0 and S % SB_QBLK == 0 and (SB_QBLK // SB_KBLK) % SB_BLOCKS_PER_ITER == 0
    depth = norm_mix.shape[0]
    MP, MS = B * S, DB * T
    tm = min(512, MP)
    bf = lambda w: w.astype(BF16)

    xp = x_prompt.reshape(MP, D)
    xs = x_sample.reshape(MS, D)
    caches_a = (cache_a_w128, cache_a_w512, cache_a_w2048)
    w1_all, w2_all, w_proj_all, w_gate_all = bf(ffn_w1), bf(ffn_w2), bf(ple_w_proj), bf(ple_w_gate)
    a_w_in_all = bf(a_w_in)
    a_x, a_gain = [], []
    a_s = [[] for _ in A_WINDOWS]
    b_s_rows, c_p, c_s = [], [], []

    for i in range(depth):
        kind, j = i % N_MIXERS, i // N_MIXERS
        if kind == 0:
            w_in, w_out = a_w_in_all[j], bf(a_w_out[j])
            qkv_p = group_matmul(norm_streams(xp, norm_mix[i]), a_w_in_all, j, A_TILE)
            outs, lses = zip(*[dil_attn(qkv_p, g, dil, S // A_TILE) for g, dil in enumerate(A_DILATIONS)])
            a_x.append(xp.reshape(B, S, D))
            a_gain.append(norm_mix[i])
            xp = merge_outproj(xp, outs, lses, w_out, tm)
            qkv_s = norm_matmul(xs, norm_mix[i], w_in, None, F32, MS, 1024)
            xs = outproj(xs, dil_sample(qkv_s, caches_a, j), w_out, MS)
            kv_s = qkv_s.reshape(DB, T, 3, A_GROUPS, HEADS, HEAD_DIM)
            for g in range(A_GROUPS):
                a_s[g].append(jnp.stack([kv_s[:, :, 1, g], kv_s[:, :, 2, g]], axis=2))
        elif kind == 1:
            w_in, w_out = bf(b_w_in[j]), bf(b_w_out[j])
            xp = sgu_prompt(xp, norm_mix[i], w_in, b_ln_g[j], b_ln_b[j], b_w_s[j], b_b_s[j], w_out, tm)
            xs, v_s = sgu_sample(xs, norm_mix[i], w_in, b_ln_g[j], b_ln_b[j], b_w_s[j], b_b_s[j], w_out)
            b_s_rows.append(v_s.reshape(DB, T, -1))
        else:
            w_in, w_out = bf(c_w_in[j]), bf(c_w_out[j])
            qkv_p = norm_matmul(xp, norm_mix[i], w_in, c_b_in[j], BF16, tm, 3 * INNER)
            kv_t = norm_matmul_t([xp.reshape(B, S, D)], norm_mix[i][None], jnp.transpose(w_in[:, INNER:])[None],
                                 c_b_in[j][None, INNER:], S)
            c_p.append(kv_rows_from_t(kv_t)[0])
            qkv_s = norm_matmul(xs, norm_mix[i], w_in, c_b_in[j], F32, MS, 1024)
            o_p = sb_attn(qkv_p.reshape(B, S, -1))
            xp = outproj(xp, o_p.reshape(MP, INNER), w_out, tm)
            xs = outproj(xs, sb_sample(qkv_s[:, :INNER], cache_c_kv, j, page_table), w_out, MS)
            c_s.append(qkv_s[:, INNER:].reshape(DB, T, 2, HEADS, HEAD_DIM))
        g_final = norm_final if i == depth - 1 else None
        tail = (norm_ffn[i], w1_all, w2_all, norm_ple[i], w_proj_all, w_gate_all, g_final)
        xp = ffn_ple(xp, i, p_prompt.reshape(depth, MP, -1), *tail, tm)
        xs = ffn_ple(xs, i, p_sample.reshape(depth, MS, -1), *tail, MS)

    w5 = a_w_in_all.reshape(-1, D, 3, A_GROUPS, INNER)
    a_p = []
    for g, win in enumerate(A_WINDOWS):
        wt_kv = jnp.transpose(w5[:, :, 1:, g, :].reshape(-1, D, 2 * INNER), (0, 2, 1))
        a_p.append(kv_rows_from_t(norm_matmul_t(a_x, jnp.stack(a_gain), wt_kv, None, min(win, S))))

    return (xp.reshape(B, S, D), xs.reshape(DB, T, D),
            a_p[0], a_p[1], a_p[2],
            jnp.stack(a_s[0]), jnp.stack(a_s[1]), jnp.stack(a_s[2]),
            jnp.stack(b_s_rows), jnp.stack(c_p), jnp.stack(c_s))
```

```python
import functools
import math

import jax
import jax.numpy as jnp
from jax import lax
from jax.experimental import pallas as pl
from jax.experimental.pallas import tpu as pltpu

F32 = jnp.float32
BF16 = jnp.bfloat16

HEAD_DIM = 64
HEADS = 16
INNER = HEADS * HEAD_DIM
ATTN_SCALE = 1.0 / math.sqrt(HEAD_DIM)
LOG2E = 1.4426950408889634
BLK = 128
A_WINDOWS = (128, 512, 2048)
A_DILATIONS = (1, 4, 16)
A_GROUPS = 3
A_TILE = 2048
A_UNITS = A_TILE // BLK
B_GROUPS = 8
CHUNK = 128
SB_QBLK = 1024
SB_KBLK = 256
SB_BLOCKS_PER_ITER = 4
SB_PAGES_PER_STEP = 8
FFN_CHUNK = 512
SAMPLE_HEAD_GROUPS = 4
N_MIXERS = 3
EPS = 1e-6
LANES = 128
SLABS = INNER // LANES
VMEM_LIMIT = 56 * 1024 * 1024


def _params(*sem):
    return pltpu.CompilerParams(dimension_semantics=sem, vmem_limit_bytes=VMEM_LIMIT)


def _rms(xf, g):
    return xf * lax.rsqrt(jnp.mean(xf * xf, axis=-1, keepdims=True) + EPS) * g


def _dot(a, b):
    return jnp.dot(a, b, preferred_element_type=F32)


def _dot_nt(a, b):
    return lax.dot_general(a, b, (((1,), (1,)), ((), ())), preferred_element_type=F32)


def _dot_split(a, b_bf16):
    rows = a.shape[0]
    hi = a.astype(BF16)
    lo = (a - hi.astype(F32)).astype(BF16)
    both = _dot(jnp.concatenate([hi, lo], axis=0), b_bf16)
    return both[:rows] + both[rows:]


def _softplus(z):
    return jnp.maximum(z, 0.0) + jnp.log(1.0 + jnp.exp2(jnp.abs(z) * (-LOG2E)))


def _later(n):
    return (lax.broadcasted_iota(jnp.int32, (n, n), 0) > lax.broadcasted_iota(jnp.int32, (n, n), 1)).astype(BF16)


def _eye(n):
    return lax.broadcasted_iota(jnp.int32, (n, n), 0) == lax.broadcasted_iota(jnp.int32, (n, n), 1)


def _row_to_col(row):
    n = row.shape[1]
    return jnp.sum(jnp.where(_eye(n), row, 0.0), axis=1, keepdims=True)


def _col_to_row(col):
    n = col.shape[0]
    return jnp.sum(jnp.where(_eye(n), col, 0.0), axis=0, keepdims=True)


def _head_sum(x, heads):
    return jnp.sum(x.reshape(heads, HEAD_DIM, x.shape[1]), axis=1)


def _head_rows(x, heads):
    return jnp.broadcast_to(x[:, None, :], (heads, HEAD_DIM, x.shape[1])).reshape(heads * HEAD_DIM, x.shape[1])


def _norm_matmul_kernel(x_ref, g_ref, w_ref, *rest, has_bias):
    if has_bias:
        b_ref, o_ref, h_ref = rest
    else:
        o_ref, h_ref = rest

    @pl.when(pl.program_id(1) == 0)
    def _():
        h_ref[...] = _rms(x_ref[...], g_ref[...]).astype(BF16)

    acc = _dot(h_ref[...], w_ref[...])
    if has_bias:
        acc = acc + b_ref[...]
    o_ref[...] = acc.astype(o_ref.dtype)


def norm_matmul(x, g, w, bias, out_dtype, tm, tn):
    M, D = x.shape
    N = w.shape[1]
    in_specs = [pl.BlockSpec((tm, D), lambda i, j: (i, 0)),
                pl.BlockSpec((1, D), lambda i, j: (0, 0)),
                pl.BlockSpec((D, tn), lambda i, j: (0, j))]
    args = [x, g.reshape(1, D), w]
    if bias is not None:
        in_specs.append(pl.BlockSpec((1, tn), lambda i, j: (0, j)))
        args.append(bias.reshape(1, N))
    return pl.pallas_call(
        functools.partial(_norm_matmul_kernel, has_bias=bias is not None),
        grid=(M // tm, N // tn),
        in_specs=in_specs,
        out_specs=pl.BlockSpec((tm, tn), lambda i, j: (i, j)),
        out_shape=jax.ShapeDtypeStruct((M, N), out_dtype),
        scratch_shapes=[pltpu.VMEM((tm, D), BF16)],
        compiler_params=_params("arbitrary", "arbitrary"),
        name="norm_matmul",
    )(*args)


def _norm_matmul_t_kernel(*refs, n_layers, has_bias):
    x_refs = refs[:n_layers]
    g_ref, wt_ref = refs[n_layers:n_layers + 2]
    if has_bias:
        b_ref, o_ref = refs[n_layers + 2:]
    else:
        (o_ref,) = refs[n_layers + 2:]
    for layer, x_ref in enumerate(x_refs):
        @pl.when(pl.program_id(0) == layer)
        def _(x_ref=x_ref):
            h = _rms(x_ref[...], g_ref[...]).astype(BF16)
            acc = _dot_nt(wt_ref[...], h)
            if has_bias:
                acc = acc + b_ref[...]
            o_ref[...] = acc


def norm_matmul_t(xs, g, wt, bias, keep):
    n_layers = len(xs)
    B, S, D = xs[0].shape
    N = wt.shape[1]
    tm = min(512, keep)
    first = (S - keep) // tm
    in_specs = [pl.BlockSpec((None, tm, D), lambda l, b, i: (b, first + i, 0))] * n_layers
    in_specs += [pl.BlockSpec((None, 1, D), lambda l, b, i: (l, 0, 0)),
                 pl.BlockSpec((None, N, D), lambda l, b, i: (l, 0, 0))]
    args = list(xs) + [g.reshape(n_layers, 1, D), wt]
    if bias is not None:
        in_specs.append(pl.BlockSpec((None, N, 1), lambda l, b, i: (l, 0, 0)))
        args.append(bias.reshape(n_layers, N, 1))
    return pl.pallas_call(
        functools.partial(_norm_matmul_t_kernel, n_layers=n_layers, has_bias=bias is not None),
        grid=(n_layers, B, keep // tm),
        in_specs=in_specs,
        out_specs=pl.BlockSpec((None, None, N, tm), lambda l, b, i: (l, b, 0, i)),
        out_shape=jax.ShapeDtypeStruct((n_layers, B, N, keep), F32),
        compiler_params=_params("arbitrary", "arbitrary", "arbitrary"),
        name="norm_matmul_t",
    )(*args)


def kv_rows_from_t(kv_t):
    L, B, _, keep = kv_t.shape
    return jnp.transpose(kv_t.reshape(L, B, 2, HEADS, HEAD_DIM, keep), (0, 1, 5, 2, 3, 4))


def _merge_outproj_kernel(x_ref, o0_ref, o1_ref, o2_ref, l0_ref, l1_ref, l2_ref, w_ref, y_ref):
    l0, l1, l2 = l0_ref[...], l1_ref[...], l2_ref[...]
    m = jnp.maximum(jnp.maximum(l0, l1), l2)
    e0, e1, e2 = jnp.exp(l0 - m), jnp.exp(l1 - m), jnp.exp(l2 - m)
    den = e0 + e1 + e2
    expand = (lax.broadcasted_iota(jnp.int32, (LANES, INNER), 1) // HEAD_DIM
              == lax.broadcasted_iota(jnp.int32, (LANES, INNER), 0)).astype(BF16)
    inv = 1.0 / den
    o = (_dot((e0 * inv).astype(BF16), expand) * o0_ref[...].astype(F32)
         + _dot((e1 * inv).astype(BF16), expand) * o1_ref[...].astype(F32)
         + _dot((e2 * inv).astype(BF16), expand) * o2_ref[...].astype(F32))
    y_ref[...] = x_ref[...] + _dot(o.astype(BF16), w_ref[...])


def merge_outproj(x, outs, lses, w, tm):
    M, D = x.shape
    row = lambda width: pl.BlockSpec((tm, width), lambda i: (i, 0))
    return pl.pallas_call(
        _merge_outproj_kernel,
        grid=(M // tm,),
        in_specs=[row(D)] + [row(INNER)] * A_GROUPS + [row(LANES)] * A_GROUPS
        + [pl.BlockSpec((INNER, D), lambda i: (0, 0))],
        out_specs=row(D),
        out_shape=jax.ShapeDtypeStruct((M, D), F32),
        compiler_params=_params("arbitrary"),
        name="merge_outproj",
    )(x, *outs, *lses, w)


def _norm_streams_kernel(x_ref, g_ref, h_ref, slab_ref):
    grp = pl.program_id(1)

    @pl.when(grp == 0)
    def _():
        rows = 256
        for c in range(A_TILE // rows):
            sl = slice(c * rows, (c + 1) * rows)
            h = _rms(x_ref[sl, :], g_ref[...])
            h_ref[sl, :] = h.astype(BF16)
            for k in range(SLABS):
                slab_ref[k, sl, :] = h[:, k * LANES:(k + 1) * LANES]

    for g, dil in enumerate(A_DILATIONS):
        if dil == 1:
            continue
        per_stream = A_TILE // dil

        @pl.when(grp == g)
        def _(dil=dil, per_stream=per_stream):
            for r in range(dil):
                for k in range(SLABS):
                    h_ref[r * per_stream:(r + 1) * per_stream, k * LANES:(k + 1) * LANES] = (
                        slab_ref[k, pl.ds(r, per_stream, stride=dil), :].astype(BF16))


def norm_streams(x, g):
    M, D = x.shape
    assert A_DILATIONS[0] == 1
    return pl.pallas_call(
        _norm_streams_kernel,
        grid=(M // A_TILE, A_GROUPS),
        in_specs=[pl.BlockSpec((A_TILE, D), lambda i, g: (i, 0)), pl.BlockSpec((1, D), lambda i, g: (0, 0))],
        out_specs=pl.BlockSpec((None, A_TILE, D), lambda i, g: (g, i, 0)),
        out_shape=jax.ShapeDtypeStruct((A_GROUPS, M, D), BF16),
        scratch_shapes=[pltpu.VMEM((SLABS, A_TILE, LANES), F32)],
        compiler_params=_params("arbitrary", "arbitrary"),
        name="norm_streams",
    )(x, g.reshape(1, D))


def _group_matmul_kernel(h_ref, w_ref, o_ref):
    o_ref[...] = _dot(h_ref[...], w_ref[...]).astype(o_ref.dtype)


def group_matmul(h, w, layer, tm):
    G, M, D = h.shape
    P = w.shape[2] // (G * INNER)
    return pl.pallas_call(
        _group_matmul_kernel,
        grid=(G, M // tm, P),
        in_specs=[pl.BlockSpec((None, tm, D), lambda g, i, j: (g, i, 0)),
                  pl.BlockSpec((None, D, INNER), lambda g, i, j: (layer, 0, j * G + g))],
        out_specs=pl.BlockSpec((None, None, tm, INNER), lambda g, i, j: (g, j, i, 0)),
        out_shape=jax.ShapeDtypeStruct((G, P, M, INNER), BF16),
        compiler_params=_params("arbitrary", "arbitrary", "arbitrary"),
        name="group_matmul",
    )(h, w)


def _dil_attn_kernel(q_ref, kp_ref, kc_ref, vp_ref, vc_ref, o_ref, lse_ref, o_slab, lse_slab, *,
                     dil, tiles_per_seq):
    per_stream = A_UNITS // dil
    u = pl.program_id(1)
    r, t = u // per_stream, u % per_stream
    has_prev = (t > 0) | (pl.program_id(0) % tiles_per_seq > 0)
    row = lax.broadcasted_iota(jnp.int32, (BLK, 2 * BLK), 0)
    col = lax.broadcasted_iota(jnp.int32, (BLK, 2 * BLK), 1) & (BLK - 1)
    mask_c = col <= row
    mask_p = (col >= row) & has_prev
    lane = lax.broadcasted_iota(jnp.int32, (BLK, LANES), 1)
    first_head = lane < HEAD_DIM
    rows2 = lax.broadcasted_iota(jnp.int32, (2 * BLK, LANES), 0)
    lanes2 = lax.broadcasted_iota(jnp.int32, (2 * BLK, LANES), 1)
    ones_sel = ((rows2 < BLK) == (lanes2 < HEAD_DIM)).astype(BF16)
    lse_tile = jnp.zeros((BLK, LANES), F32)
    start = r + dil * BLK * t

    def two_heads(x):
        zero = jnp.zeros_like(x)
        return jnp.concatenate([jnp.where(first_head, x, zero), jnp.where(first_head, zero, x)], axis=0)

    for pair in range(HEADS // 2):
        sl = slice(pair * LANES, (pair + 1) * LANES)
        q2 = q_ref[:, sl] * ATTN_SCALE
        s_c = jnp.where(mask_c, _dot_nt(q2, two_heads(kc_ref[:, sl])), -jnp.inf)
        s_p = jnp.where(mask_p, _dot_nt(q2, two_heads(kp_ref[:, sl])), -jnp.inf)
        s_max = jnp.maximum(s_c, s_p)
        m0 = jnp.max(s_max[:, :BLK], axis=1, keepdims=True)
        m1 = jnp.max(s_max[:, BLK:], axis=1, keepdims=True)
        m = jnp.concatenate([jnp.broadcast_to(m0, (BLK, BLK)), jnp.broadcast_to(m1, (BLK, BLK))], axis=1)
        e_c = jnp.exp(s_c - m).astype(BF16)
        e_p = jnp.exp(s_p - m).astype(BF16)
        v_c = jnp.concatenate([two_heads(vc_ref[:, sl]), ones_sel], axis=1)
        v_p = jnp.concatenate([two_heads(vp_ref[:, sl]), ones_sel], axis=1)
        res = _dot(e_c, v_c) + _dot(e_p, v_p)
        den = res[:, LANES:]
        out = res[:, :LANES] / den
        if dil == 1:
            o_slab[pair, pl.ds(start, BLK), :] = out
        else:
            o_slab[pair, pl.ds(start, BLK, stride=dil), :] = out
        lse_tile = jnp.where(lane == 2 * pair, m0 + jnp.log(den[:, :1]), lse_tile)
        lse_tile = jnp.where(lane == 2 * pair + 1, m1 + jnp.log(den[:, HEAD_DIM:HEAD_DIM + 1]), lse_tile)
    if dil == 1:
        lse_slab[pl.ds(start, BLK), :] = lse_tile
    else:
        lse_slab[pl.ds(start, BLK, stride=dil), :] = lse_tile

    @pl.when(u == A_UNITS - 1)
    def _():
        for k in range(SLABS):
            o_ref[:, k * LANES:(k + 1) * LANES] = o_slab[k].astype(o_ref.dtype)
        lse_ref[...] = lse_slab[...]


def dil_attn(qkv, g, dil, tiles_per_seq):
    M = qkv.shape[2]
    per_stream = A_UNITS // dil

    def cur(part):
        return pl.BlockSpec((None, None, BLK, INNER), lambda c, u: (g, part, c * A_UNITS + u, 0))

    def prev(part):
        def index(c, u):
            blk = c * A_UNITS + u
            t = u % per_stream
            in_tile = blk - 1
            in_prev_tile = blk - A_UNITS + per_stream - 1
            has_prev_tile = c % tiles_per_seq > 0
            return (g, part, jnp.where(t > 0, in_tile, jnp.where(has_prev_tile, in_prev_tile, blk)), 0)
        return pl.BlockSpec((None, None, BLK, INNER), index)

    return pl.pallas_call(
        functools.partial(_dil_attn_kernel, dil=dil, tiles_per_seq=tiles_per_seq),
        grid=(M // A_TILE, A_UNITS),
        in_specs=[cur(0), prev(1), cur(1), prev(2), cur(2)],
        out_specs=[pl.BlockSpec((A_TILE, INNER), lambda c, u: (c, 0)),
                   pl.BlockSpec((A_TILE, LANES), lambda c, u: (c, 0))],
        out_shape=[jax.ShapeDtypeStruct((M, INNER), BF16), jax.ShapeDtypeStruct((M, LANES), F32)],
        scratch_shapes=[pltpu.VMEM((SLABS, A_TILE, LANES), F32), pltpu.VMEM((A_TILE, LANES), F32)],
        compiler_params=_params("arbitrary", "arbitrary"),
        name=f"dil_attn_d{dil}",
    )(qkv, qkv, qkv, qkv, qkv)


def _dil_sample_kernel(*refs):
    new_refs = refs[:3 * A_GROUPS]
    cache_refs = refs[3 * A_GROUPS:5 * A_GROUPS]
    o_ref = refs[5 * A_GROUPS]
    heads = HEADS // SAMPLE_HEAD_GROUPS
    scores, new_scores = [], []
    for g, dil in enumerate(A_DILATIONS):
        q_col = _row_to_col(new_refs[g][...])
        k_col = _row_to_col(new_refs[A_GROUPS + g][...])
        kt = cache_refs[2 * g][...]
        s = _head_sum(kt * q_col, heads) * ATTN_SCALE
        pos = lax.broadcasted_iota(jnp.int32, s.shape, 1)
        scores.append(jnp.where((pos & (dil - 1)) == 0, s, -jnp.inf))
        new_scores.append(_head_sum(q_col * k_col, heads) * ATTN_SCALE)
    m = new_scores[0]
    for g in range(A_GROUPS):
        m = jnp.maximum(m, jnp.maximum(new_scores[g], jnp.max(scores[g], axis=1, keepdims=True)))
    num = jnp.zeros((heads * HEAD_DIM, 1), F32)
    den = jnp.zeros((heads, 1), F32)
    for g in range(A_GROUPS):
        v_col = _row_to_col(new_refs[2 * A_GROUPS + g][...])
        e = jnp.exp(scores[g] - m)
        e_new = jnp.exp(new_scores[g] - m)
        vt = cache_refs[2 * g + 1][...]
        num = num + jnp.sum(vt * _head_rows(e, heads), axis=1, keepdims=True) + _head_rows(e_new, heads) * v_col
        den = den + jnp.sum(e, axis=1, keepdims=True) + e_new
    o_ref[...] = _col_to_row(num / _head_rows(den, heads))


def dil_sample(qkv, caches, layer):
    DB, W = qkv.shape
    hgs = SAMPLE_HEAD_GROUPS
    width = INNER // hgs
    new_specs = [pl.BlockSpec((None, 1, width), lambda b, h, c=(part * A_GROUPS + g) * hgs: (b, 0, c + h))
                 for part in range(3) for g in range(A_GROUPS)]
    views, specs = [], []
    for cache, win, dil in zip(caches, A_WINDOWS, A_DILATIONS):
        assert cache.shape[2] == win and win // dil == BLK and dil & (dil - 1) == 0
        view = jnp.transpose(cache, (0, 1, 3, 4, 5, 2)).reshape(cache.shape[0], DB, 2 * INNER, win)
        views += [view, view]
        specs += [pl.BlockSpec((None, None, width, win), lambda b, h: (layer, b, h, 0)),
                  pl.BlockSpec((None, None, width, win), lambda b, h: (layer, b, hgs + h, 0))]
    qkv3 = qkv.reshape(DB, 1, W)
    o = pl.pallas_call(
        _dil_sample_kernel,
        grid=(DB, hgs),
        in_specs=new_specs + specs,
        out_specs=pl.BlockSpec((None, 1, width), lambda b, h: (b, 0, h)),
        out_shape=jax.ShapeDtypeStruct((DB, 1, INNER), F32),
        compiler_params=_params("arbitrary", "arbitrary"),
        name="dil_sample",
    )(*([qkv3] * (3 * A_GROUPS)), *views)
    return o.reshape(DB, INNER)


def _sgu_front(x, g_ref, win_ref, lng_ref, lnb_ref):
    z = jax.nn.gelu(_dot(_rms(x, g_ref[...]).astype(BF16), win_ref[...]))
    width = z.shape[1] // 2
    u, vr = z[:, :width], z[:, width:]
    mu = jnp.mean(vr, axis=-1, keepdims=True)
    var = jnp.mean(jnp.square(vr - mu), axis=-1, keepdims=True)
    return u, (vr - mu) * lax.rsqrt(var + EPS) * lng_ref[...] + lnb_ref[...]


def _sgu_prompt_kernel(x_ref, g_ref, win_ref, lng_ref, lnb_ref, ws_ref, bs_ref, wout_ref, y_ref, us_ref):
    x = x_ref[...]
    u, v = _sgu_front(x, g_ref, win_ref, lng_ref, lnb_ref)
    causal = (lax.broadcasted_iota(jnp.int32, (CHUNK, CHUNK), 0)
              >= lax.broadcasted_iota(jnp.int32, (CHUNK, CHUNK), 1))
    for grp in range(B_GROUPS):
        cols = slice(grp * CHUNK, (grp + 1) * CHUNK)
        wm = jnp.where(causal, ws_ref[grp], 0.0).astype(BF16)
        for c in range(x.shape[0] // CHUNK):
            rows = slice(c * CHUNK, (c + 1) * CHUNK)
            s = _dot(wm, v[rows, cols].astype(BF16)) + bs_ref[:, cols]
            us_ref[rows, cols] = (u[rows, cols] * s).astype(BF16)
    y_ref[...] = x + _dot(us_ref[...], wout_ref[...])


def sgu_prompt(x, g, w_in, ln_g, ln_b, w_s, b_s, w_out, tm):
    M, D = x.shape
    width = w_out.shape[0]
    bias = jnp.repeat(b_s.T, width // B_GROUPS, axis=1)
    const = lambda shape: pl.BlockSpec(shape, lambda i: (0,) * len(shape))
    return pl.pallas_call(
        _sgu_prompt_kernel,
        grid=(M // tm,),
        in_specs=[pl.BlockSpec((tm, D), lambda i: (i, 0)), const((1, D)), const((D, 2 * width)),
                  const((1, width)), const((1, width)), const((B_GROUPS, CHUNK, CHUNK)),
                  const((CHUNK, width)), const((width, D))],
        out_specs=pl.BlockSpec((tm, D), lambda i: (i, 0)),
        out_shape=jax.ShapeDtypeStruct((M, D), F32),
        scratch_shapes=[pltpu.VMEM((tm, width), BF16)],
        compiler_params=_params("arbitrary"),
        name="sgu_prompt",
    )(x, g.reshape(1, D), w_in, ln_g.reshape(1, width), ln_b.reshape(1, width), w_s, bias, w_out)


def _sgu_sample_kernel(x_ref, g_ref, win_ref, lng_ref, lnb_ref, sw_ref, sb_ref, wout_ref, y_ref, v_ref):
    x = x_ref[...]
    u, v = _sgu_front(x, g_ref, win_ref, lng_ref, lnb_ref)
    s = sw_ref[...] * v + sb_ref[...]
    y_ref[...] = x + _dot((u * s).astype(BF16), wout_ref[...])
    v_ref[...] = v


def sgu_sample(x, g, w_in, ln_g, ln_b, w_s, b_s, w_out):
    M, D = x.shape
    width = w_out.shape[0]
    sw = jnp.repeat(w_s[:, 0, 0], width // B_GROUPS).reshape(1, width)
    sb = jnp.repeat(b_s[:, 0], width // B_GROUPS).reshape(1, width)
    return pl.pallas_call(
        _sgu_sample_kernel,
        out_shape=[jax.ShapeDtypeStruct((M, D), F32), jax.ShapeDtypeStruct((M, width), F32)],
        compiler_params=pltpu.CompilerParams(vmem_limit_bytes=VMEM_LIMIT),
        name="sgu_sample",
    )(x, g.reshape(1, D), w_in, ln_g.reshape(1, width), ln_b.reshape(1, width), sw, sb, w_out)


def _sb_attn_kernel(q_ref, k_ref, v_ref, o_ref):
    i = pl.program_id(2)
    nq, nk = SB_QBLK, SB_KBLK
    lane = lax.broadcasted_iota(jnp.int32, (nq, LANES), 1)
    first_head = lane < HEAD_DIM
    later = _later(nk)
    q2 = q_ref[...] * ATTN_SCALE
    zero = jnp.zeros_like(q2)
    qs = jnp.concatenate([jnp.where(first_head, q2, zero), jnp.where(first_head, zero, q2)], axis=0)

    def rows_from(x, lo):
        return x if lo == 0 else jnp.concatenate([x[lo:nq], x[nq + lo:]], axis=0)

    def weights(kb, c, lo):
        start = pl.multiple_of(kb * nk, nk)
        first = 0 if lo is None else lo
        n = nq - first
        z = _dot_nt(rows_from(qs, first), k_ref[pl.ds(start, nk), :])
        sp = _softplus(z)
        if lo is not None:
            r = lax.broadcasted_iota(jnp.int32, (2 * n, nk), 0)
            valid = lax.broadcasted_iota(jnp.int32, (2 * n, nk), 1) < jnp.where(r < n, r, r - n)
            sp = jnp.where(valid, sp, 0.0)
        cs = _dot(sp.astype(BF16), later)
        a = jnp.exp(z - sp - cs - rows_from(c, first))
        if lo is not None:
            a = jnp.where(valid, a, 0.0)
        return a.astype(BF16), cs[:, :1] + sp[:, :1], start

    def diagonal_block(kb, o, c, lo):
        a, dc, start = weights(kb, c, lo)
        do = _dot(a, v_ref[pl.ds(start, nk), :])
        if lo == 0:
            return o + do, c + dc
        n = nq - lo
        pad = lambda d: jnp.concatenate(
            [jnp.zeros((lo, d.shape[1]), F32), d[:n], jnp.zeros((lo, d.shape[1]), F32), d[n:]], axis=0)
        return o + pad(do), c + pad(dc)

    def earlier_blocks(s, oc):
        o, c = oc
        kb = first_kb - 1 - SB_BLOCKS_PER_ITER * s
        do = None
        for j in range(SB_BLOCKS_PER_ITER):
            a, dc, start = weights(kb - j, c, None)
            c = c + dc
            part = _dot(a, v_ref[pl.ds(start, nk), :])
            do = part if do is None else do + part
        return o + do, c

    o = jnp.zeros((2 * nq, LANES), F32)
    c = jnp.zeros((2 * nq, 1), F32)
    first_kb = i * (nq // nk)
    for d in reversed(range(nq // nk)):
        o, c = diagonal_block(first_kb + d, o, c, d * nk)
    o, c = lax.fori_loop(0, first_kb // SB_BLOCKS_PER_ITER, earlier_blocks, (o, c))
    o_ref[...] = jnp.where(first_head, o[:nq], o[nq:]).astype(o_ref.dtype)


def sb_attn(qkv):
    B, S, _ = qkv.shape
    pairs = INNER // LANES
    return pl.pallas_call(
        _sb_attn_kernel,
        grid=(B, pairs, S // SB_QBLK),
        in_specs=[pl.BlockSpec((None, SB_QBLK, LANES), lambda b, p, i: (b, i, p)),
                  pl.BlockSpec((None, S, LANES), lambda b, p, i: (b, 0, pairs + p)),
                  pl.BlockSpec((None, S, LANES), lambda b, p, i: (b, 0, 2 * pairs + p))],
        out_specs=pl.BlockSpec((None, SB_QBLK, LANES), lambda b, p, i: (b, i, p)),
        out_shape=jax.ShapeDtypeStruct((B, S, INNER), BF16),
        compiler_params=_params("arbitrary", "arbitrary", "arbitrary"),
        name="sb_attn",
    )(qkv, qkv, qkv)


def _sb_sample_kernel(pt_ref, q_ref, *refs):
    del pt_ref
    page_refs = refs[:SB_PAGES_PER_STEP]
    o_ref, qcol_ref, acc_ref, c_ref = refs[SB_PAGES_PER_STEP:]
    p = pl.program_id(1)
    n = page_refs[0].shape[1]

    @pl.when(p == 0)
    def _():
        for k in range(SLABS):
            rows = slice(k * LANES, (k + 1) * LANES)
            qcol_ref[rows, :] = jnp.broadcast_to(_row_to_col(q_ref[:, rows]), (LANES, n))
        acc_ref[...] = jnp.zeros_like(acc_ref)
        c_ref[...] = jnp.zeros_like(c_ref)

    later = _later(n)
    for page_ref in page_refs:
        z = _head_sum(page_ref[:INNER, :] * qcol_ref[...], HEADS) * ATTN_SCALE
        sp = _softplus(z)
        a = jnp.exp(z - sp - _dot_split(sp, later) - c_ref[...])
        acc_ref[...] += page_ref[INNER:, :] * _head_rows(a, HEADS)
        c_ref[...] += jnp.sum(sp, axis=1, keepdims=True)

    @pl.when(p == pl.num_programs(1) - 1)
    def _():
        for k in range(SLABS):
            rows = slice(k * LANES, (k + 1) * LANES)
            o_ref[:, rows] = _col_to_row(jnp.sum(acc_ref[rows, :], axis=1, keepdims=True))


def sb_sample(q, pool, layer, page_table):
    DB = q.shape[0]
    layers, n_pool, page = pool.shape[:3]
    n_pages = page_table.shape[1]
    pps = SB_PAGES_PER_STEP
    assert n_pages % pps == 0
    pool_t = jnp.transpose(pool, (0, 1, 3, 4, 5, 2)).reshape(layers, n_pool, 2 * INNER, page)

    def page_spec(k):
        return pl.BlockSpec((None, None, 2 * INNER, page),
                            lambda b, p, pt: (layer, pt[b, n_pages - 1 - (p * pps + k)], 0, 0))

    o = pl.pallas_call(
        _sb_sample_kernel,
        grid_spec=pltpu.PrefetchScalarGridSpec(
            num_scalar_prefetch=1,
            grid=(DB, n_pages // pps),
            in_specs=[pl.BlockSpec((None, 1, INNER), lambda b, p, pt: (b, 0, 0))]
            + [page_spec(k) for k in range(pps)],
            out_specs=pl.BlockSpec((None, 1, INNER), lambda b, p, pt: (b, 0, 0)),
            scratch_shapes=[pltpu.VMEM((INNER, page), F32), pltpu.VMEM((INNER, page), F32),
                            pltpu.VMEM((HEADS, page), F32)]),
        out_shape=jax.ShapeDtypeStruct((DB, 1, INNER), F32),
        compiler_params=_params("arbitrary", "arbitrary"),
        name="sb_sample",
    )(page_table, q.reshape(DB, 1, INNER), *([pool_t] * pps))
    return o.reshape(DB, INNER)


def _ffn_ple_kernel(x_ref, p_ref, gf_ref, w1_ref, w2_ref, gp_ref, wproj_ref, wgate_ref, *rest, mixed, final):
    rest = list(rest)
    if mixed:
        o_ref, wo_ref = rest.pop(0), rest.pop(0)
    if final:
        gfin_ref = rest.pop(0)
    y_ref, h_ref, acc_ref = rest
    x = x_ref[...]
    if mixed:
        x = x + _dot(o_ref[...].astype(BF16), wo_ref[...])
    h_ref[...] = _rms(x, gf_ref[...]).astype(BF16)
    hidden = w1_ref.shape[1]
    for c in range(hidden // FFN_CHUNK):
        cols = slice(c * FFN_CHUNK, (c + 1) * FFN_CHUNK)
        a = jnp.square(jnp.maximum(_dot(h_ref[...], w1_ref[:, cols]), 0.0))
        part = _dot(a.astype(BF16), w2_ref[cols, :])
        if c == 0:
            acc_ref[...] = part
        else:
            acc_ref[...] += part
    x2 = x + acc_ref[...]
    gate = jax.nn.sigmoid(_dot(_rms(x2, gp_ref[...]).astype(BF16), wgate_ref[...]))
    y = x2 + _dot(p_ref[...].astype(BF16), wproj_ref[...]) * gate
    if final:
        y = _rms(y, gfin_ref[...])
    y_ref[...] = y


def ffn_ple(x, layer, p, g_ffn, w1, w2, g_ple, w_proj, w_gate, g_final, tm, mixer=None):
    M, D = x.shape
    F = w1.shape[2]
    P = p.shape[2]
    const = lambda shape: pl.BlockSpec(shape, lambda i: (0, 0))
    in_specs = [pl.BlockSpec((tm, D), lambda i: (i, 0)),
                pl.BlockSpec((None, tm, P), lambda i: (layer, i, 0)),
                const((1, D)),
                pl.BlockSpec((None, D, F), lambda i: (layer, 0, 0)),
                pl.BlockSpec((None, F, D), lambda i: (layer, 0, 0)),
                const((1, D)),
                pl.BlockSpec((None, P, D), lambda i: (layer, 0, 0)),
                pl.BlockSpec((None, D, D), lambda i: (layer, 0, 0))]
    args = [x, p, g_ffn.reshape(1, D), w1, w2, g_ple.reshape(1, D), w_proj, w_gate]
    if mixer is not None:
        o, w_out = mixer
        in_specs += [pl.BlockSpec((tm, o.shape[1]), lambda i: (i, 0)), const(w_out.shape)]
        args += [o, w_out]
    if g_final is not None:
        in_specs.append(const((1, D)))
        args.append(g_final.reshape(1, D))
    return pl.pallas_call(
        functools.partial(_ffn_ple_kernel, mixed=mixer is not None, final=g_final is not None),
        grid=(M // tm,),
        in_specs=in_specs,
        out_specs=pl.BlockSpec((tm, D), lambda i: (i, 0)),
        out_shape=jax.ShapeDtypeStruct((M, D), F32),
        scratch_shapes=[pltpu.VMEM((tm, D), BF16), pltpu.VMEM((tm, D), F32)],
        compiler_params=_params("arbitrary"),
        name="ffn_ple",
    )(*args)


def kernel(x_prompt, x_sample, p_prompt, p_sample, cache_a_w128, cache_a_w512, cache_a_w2048, cache_c_kv, page_table, norm_mix, norm_ffn, norm_ple, norm_final, a_w_in, a_w_out, b_w_in, b_ln_g, b_ln_b, b_w_s, b_b_s, b_w_out, c_w_in, c_b_in, c_w_out, ffn_w1, ffn_w2, ple_w_proj, ple_w_gate):
    B, S, D = x_prompt.shape
    DB, T, _ = x_sample.shape
    assert T == 1, "the sample path handles one new token per sequence"
    assert S % A_TILE == 0 and S % SB_QBLK == 0 and (SB_QBLK // SB_KBLK) % SB_BLOCKS_PER_ITER == 0
    depth = norm_mix.shape[0]
    MP, MS = B * S, DB * T
    tm = min(512, MP)
    bf = lambda w: w.astype(BF16)

    xp = x_prompt.reshape(MP, D)
    xs = x_sample.reshape(MS, D)
    caches_a = (cache_a_w128, cache_a_w512, cache_a_w2048)
    w1_all, w2_all, w_proj_all, w_gate_all = bf(ffn_w1), bf(ffn_w2), bf(ple_w_proj), bf(ple_w_gate)
    a_w_in_all = bf(a_w_in)
    a_x, a_gain = [], []
    a_s = [[] for _ in A_WINDOWS]
    b_s_rows, c_p, c_s = [], [], []

    for i in range(depth):
        kind, j = i % N_MIXERS, i // N_MIXERS
        mix_p = mix_s = None
        if kind == 0:
            w_in, w_out = a_w_in_all[j], bf(a_w_out[j])
            qkv_p = group_matmul(norm_streams(xp, norm_mix[i]), a_w_in_all, j, A_TILE)
            outs, lses = zip(*[dil_attn(qkv_p, g, dil, S // A_TILE) for g, dil in enumerate(A_DILATIONS)])
            a_x.append(xp.reshape(B, S, D))
            a_gain.append(norm_mix[i])
            xp = merge_outproj(xp, outs, lses, w_out, tm)
            qkv_s = norm_matmul(xs, norm_mix[i], w_in, None, F32, MS, 1024)
            mix_s = (dil_sample(qkv_s, caches_a, j), w_out)
            kv_s = qkv_s.reshape(DB, T, 3, A_GROUPS, HEADS, HEAD_DIM)
            for g in range(A_GROUPS):
                a_s[g].append(jnp.stack([kv_s[:, :, 1, g], kv_s[:, :, 2, g]], axis=2))
        elif kind == 1:
            w_in, w_out = bf(b_w_in[j]), bf(b_w_out[j])
            xp = sgu_prompt(xp, norm_mix[i], w_in, b_ln_g[j], b_ln_b[j], b_w_s[j], b_b_s[j], w_out, tm)
            xs, v_s = sgu_sample(xs, norm_mix[i], w_in, b_ln_g[j], b_ln_b[j], b_w_s[j], b_b_s[j], w_out)
            b_s_rows.append(v_s.reshape(DB, T, -1))
        else:
            w_in, w_out = bf(c_w_in[j]), bf(c_w_out[j])
            qkv_p = norm_matmul(xp, norm_mix[i], w_in, c_b_in[j], BF16, tm, 3 * INNER)
            kv_t = norm_matmul_t([xp.reshape(B, S, D)], norm_mix[i][None], jnp.transpose(w_in[:, INNER:])[None],
                                 c_b_in[j][None, INNER:], S)
            c_p.append(kv_rows_from_t(kv_t)[0])
            qkv_s = norm_matmul(xs, norm_mix[i], w_in, c_b_in[j], F32, MS, 1024)
            o_p = sb_attn(qkv_p.reshape(B, S, -1))
            mix_p = (o_p.reshape(MP, INNER), w_out)
            mix_s = (sb_sample(qkv_s[:, :INNER], cache_c_kv, j, page_table), w_out)
            c_s.append(qkv_s[:, INNER:].reshape(DB, T, 2, HEADS, HEAD_DIM))
        g_final = norm_final if i == depth - 1 else None
        tail = (norm_ffn[i], w1_all, w2_all, norm_ple[i], w_proj_all, w_gate_all, g_final)
        xp = ffn_ple(xp, i, p_prompt.reshape(depth, MP, -1), *tail, tm, mix_p)
        xs = ffn_ple(xs, i, p_sample.reshape(depth, MS, -1), *tail, MS, mix_s)

    w5 = a_w_in_all.reshape(-1, D, 3, A_GROUPS, INNER)
    a_p = []
    for g, win in enumerate(A_WINDOWS):
        wt_kv = jnp.transpose(w5[:, :, 1:, g, :].reshape(-1, D, 2 * INNER), (0, 2, 1))
        a_p.append(kv_rows_from_t(norm_matmul_t(a_x, jnp.stack(a_gain), wt_kv, None, min(win, S))))

    return (xp.reshape(B, S, D), xs.reshape(DB, T, D),
            a_p[0], a_p[1], a_p[2],
            jnp.stack(a_s[0]), jnp.stack(a_s[1]), jnp.stack(a_s[2]),
            jnp.stack(b_s_rows), jnp.stack(c_p), jnp.stack(c_s))
```
